```python
import jax, jax.numpy as jnp
from jax import lax
import numpy as np

D_MODEL = 2048
BATCH = 4
SEQ = 2048
DEPTH = 4

CHUNK = 64
LEFT_CHUNKS = 8
N_BAND = LEFT_CHUNKS + 1
HEAD_DIM = 128
D_MIX = D_MODEL
D_A = D_MIX // 2
D_B = D_MIX - D_A
H_A = D_A // HEAD_DIM
H_B = D_B // HEAD_DIM
REL_CLIP = 128
H_IDX = 8
D_IDX = 64
TOPK_MAX = 256
Q_BLOCK = 128
D_PLE = 256
ROPE_THETA = 10000.0
LN_EPS = 1e-5
NEG = -1e30
ALPHA = (2.0 * DEPTH) ** 0.25
BETA = (8.0 * DEPTH) ** -0.25
SPLIT_SIZES = (D_A, D_A, D_A, D_A, D_B, HEAD_DIM, HEAD_DIM, D_B, H_IDX * D_IDX, D_IDX, H_IDX)
V_PARTS = (2, 6)
D_IN = sum(SPLIT_SIZES)
SPLIT_IDX = tuple(int(v) for v in np.cumsum(SPLIT_SIZES)[:-1])

kernel_name = 'hybrid_chunk_relpos_dsa_deepnorm'


def layer_norm(x, g, b):
    xf = x.astype(jnp.float32)
    mu = jnp.mean(xf, -1, keepdims=True)
    var = jnp.mean(jnp.square(xf - mu), -1, keepdims=True)
    y = (xf - mu) * lax.rsqrt(var + LN_EPS)
    return (y * g.astype(jnp.float32) + b.astype(jnp.float32)).astype(x.dtype)


def rope_tables(positions, dim):
    inv = ROPE_THETA ** (-jnp.arange(0, dim, 2, dtype=jnp.float32) / dim)
    ang = positions.astype(jnp.float32)[..., None] * inv
    ang = jnp.concatenate([ang, ang], -1)[:, :, None, :]
    return jnp.cos(ang), jnp.sin(ang)


def apply_rope(t, cos, sin):
    t1, t2 = jnp.split(t, 2, axis=-1)
    rot = jnp.concatenate([-t2, t1], -1)
    return (t.astype(jnp.float32) * cos + rot.astype(jnp.float32) * sin).astype(t.dtype)


def chunk_relpos_attention(q, k, v, rel_bias):
    Bn, S, H, Dh = q.shape
    nc = S // CHUNK
    qc = q.reshape(Bn, nc, CHUNK, H, Dh)

    def band(t):
        t = t.reshape(Bn, nc, CHUNK, H, Dh)
        tp = jnp.pad(t, ((0, 0), (LEFT_CHUNKS, 0), (0, 0), (0, 0), (0, 0)))
        return jnp.concatenate([tp[:, j:j + nc] for j in range(N_BAND)], axis=2)

    kb, vb = band(k), band(v)
    s = jnp.einsum('bnqhd,bnkhd->bnhqk', qc, kb).astype(jnp.float32) * (Dh ** -0.5)
    qi = jnp.arange(CHUNK)[:, None]
    slot = jnp.arange(N_BAND * CHUNK)[None, :]
    dist = LEFT_CHUNKS * CHUNK + qi - slot
    bias = rel_bias[:, jnp.clip(dist, -REL_CLIP, REL_CLIP) + REL_CLIP]
    key_chunk = jnp.arange(nc)[:, None] - LEFT_CHUNKS + slot // CHUNK
    s = s + bias.astype(jnp.float32)[None, None]
    s = jnp.where((key_chunk >= 0)[None, :, None, None, :], s, NEG)
    pr = jax.nn.softmax(s, axis=-1).astype(v.dtype)
    o = jnp.einsum('bnhqk,bnkhd->bnqhd', pr, vb)
    return o.reshape(Bn, S, H * Dh)


def indexed_sparse_attention(q, k, v, q_idx, k_idx, w_idx, topk):
    Bn, S = q.shape[:2]
    nb = S // Q_BLOCK
    key_pos = jnp.arange(S)

    def blockify(t):
        return jnp.moveaxis(t.reshape((Bn, nb, Q_BLOCK) + t.shape[2:]), 1, 0)

    def one_block(args):
        blk, qb, qib, wib = args
        t = blk * Q_BLOCK + jnp.arange(Q_BLOCK)
        visible_end = (t // CHUNK + 1) * CHUNK
        admissible = key_pos[None, :] < visible_end[:, None]
        dots = jnp.einsum('bqhd,bsd->bqhs', qib, k_idx).astype(jnp.float32) * (D_IDX ** -0.5)
        score = jnp.einsum('bqh,bqhs->bqs', wib.astype(jnp.float32) * (H_IDX ** -0.5), jax.nn.relu(dots))
        score = jnp.where(admissible[None], score, NEG)
        top_val, top_idx = lax.top_k(score, topk)
        sel_valid = top_val > 0.5 * NEG
        k_sel = jax.vmap(lambda kk, ii: kk[ii])(k, top_idx)
        v_sel = jax.vmap(lambda vv, ii: vv[ii])(v, top_idx)
        s = jnp.einsum('bqhd,bqkd->bhqk', qb, k_sel).astype(jnp.float32) * (HEAD_DIM ** -0.5)
        s = jnp.where(sel_valid[:, None], s, NEG)
        pr = jax.nn.softmax(s, axis=-1).astype(v.dtype)
        return jnp.einsum('bhqk,bqkd->bqhd', pr, v_sel)

    out = lax.map(one_block, (jnp.arange(nb), blockify(q), blockify(q_idx), blockify(w_idx)))
    return jnp.moveaxis(out, 0, 1).reshape(Bn, S, H_B * HEAD_DIM)


def hybrid_layer(x, p_i, rope_h, rope_i, w_in, b_in, rel_bias, w_out, w_ple, w_pg, b_pg, ln_g, ln_b, topk):
    Bn, S, _ = x.shape
    cos_h, sin_h = rope_h
    cos_i, sin_i = rope_i
    proj = x @ w_in + b_in
    qa, ka, va, ga, qb, kb, vb, gb, qi, ki, wi = jnp.split(proj, SPLIT_IDX, axis=-1)
    ya = chunk_relpos_attention(qa.reshape(Bn, S, H_A, HEAD_DIM), ka.reshape(Bn, S, H_A, HEAD_DIM),
                                va.reshape(Bn, S, H_A, HEAD_DIM), rel_bias)
    qb = apply_rope(qb.reshape(Bn, S, H_B, HEAD_DIM), cos_h, sin_h)
    kb = apply_rope(kb.reshape(Bn, S, 1, HEAD_DIM), cos_h, sin_h).reshape(Bn, S, HEAD_DIM)
    qi = apply_rope(qi.reshape(Bn, S, H_IDX, D_IDX), cos_i, sin_i)
    ki = apply_rope(ki.reshape(Bn, S, 1, D_IDX), cos_i, sin_i).reshape(Bn, S, D_IDX)
    yb = indexed_sparse_attention(qb, kb, vb, qi, ki, wi, topk)
    y = jnp.concatenate([ya * jax.nn.silu(ga), yb * jax.nn.silu(gb)], axis=-1) @ w_out
    ple = jax.nn.sigmoid(x @ w_pg + b_pg) * (p_i @ w_ple)
    return layer_norm(ALPHA * x + y + ple, ln_g, ln_b)


def setup_inputs(seed: int = 0) -> dict:
    key = jax.random.key(seed)
    ks = jax.random.split(key, 14)
    f32 = jnp.float32
    x = jax.random.normal(ks[0], (BATCH, SEQ, D_MODEL), f32)
    p = jax.random.normal(ks[1], (DEPTH, BATCH, SEQ, D_PLE), f32)
    start = jax.random.randint(ks[2], (BATCH, 1), 0, 4096, dtype=jnp.int32)
    positions = start + jnp.arange(SEQ, dtype=jnp.int32)[None, :]
    col_scale = jnp.concatenate([jnp.full((n,), BETA if i in V_PARTS else 1.0, f32)
                                 for i, n in enumerate(SPLIT_SIZES)])
    w_in = jax.random.normal(ks[3], (DEPTH, D_MODEL, D_IN), f32) * (D_MODEL ** -0.5) * col_scale
    b_in = 0.01 * jax.random.normal(ks[4], (DEPTH, D_IN), f32)
    rel_bias = 0.2 * jax.random.normal(ks[5], (DEPTH, H_A, 2 * REL_CLIP + 1), f32)
    w_out = jax.random.normal(ks[6], (DEPTH, D_MIX, D_MODEL), f32) * (D_MIX ** -0.5) * BETA
    w_ple = jax.random.normal(ks[7], (DEPTH, D_PLE, D_MODEL), f32) * (D_PLE ** -0.5) * BETA
    w_ple_gate = jax.random.normal(ks[8], (DEPTH, D_MODEL, D_MODEL), f32) * (D_MODEL ** -0.5)
    b_ple_gate = 0.01 * jax.random.normal(ks[9], (DEPTH, D_MODEL), f32)
    ln_g = 1.0 + 0.02 * jax.random.normal(ks[10], (DEPTH, D_MODEL), f32)
    ln_b = 0.01 * jax.random.normal(ks[11], (DEPTH, D_MODEL), f32)
    return {'x': x, 'p': p, 'positions': positions, 'w_in': w_in, 'b_in': b_in, 'rel_bias': rel_bias,
            'w_out': w_out, 'w_ple': w_ple, 'w_ple_gate': w_ple_gate, 'b_ple_gate': b_ple_gate,
            'ln_g': ln_g, 'ln_b': ln_b}


def reference(x, p, positions, w_in, b_in, rel_bias, w_out, w_ple, w_ple_gate, b_ple_gate, ln_g, ln_b):
    S = x.shape[1]
    topk = min(TOPK_MAX, S // 4)
    rope_h = rope_tables(positions, HEAD_DIM)
    rope_i = rope_tables(positions, D_IDX)
    for i in range(DEPTH):
        x = hybrid_layer(x, p[i], rope_h, rope_i, w_in[i], b_in[i], rel_bias[i], w_out[i], w_ple[i],
                         w_ple_gate[i], b_ple_gate[i], ln_g[i], ln_b[i], topk)
    return x
```

```python
import functools

import numpy as np
import jax
import jax.numpy as jnp
from jax import lax
from jax.experimental import pallas as pl
from jax.experimental.pallas import tpu as pltpu

D_MODEL = 2048
DEPTH = 4
CHUNK = 64
LEFT_CHUNKS = 8
HEAD_DIM = 128
D_A = 1024
D_B = 1024
H_A = 8
H_B = 8
REL_CLIP = 128
H_IDX = 8
D_IDX = 64
TOPK_MAX = 256
D_PLE = 256
ROPE_THETA = 10000.0
LN_EPS = 1e-5
NEG = -1e30
ALPHA = (2.0 * DEPTH) ** 0.25

F32 = jnp.float32
BF16 = jnp.bfloat16
LANES = 128

_SEG = {'qa': (0, 1024), 'ka': (1024, 1024), 'va': (2048, 1024), 'ga': (3072, 1024),
        'qb': (4096, 1024), 'kb': (5120, 128), 'vb': (5248, 128), 'gb': (5376, 1024),
        'qi': (6400, 512), 'ki': (6912, 64), 'wi': (6976, 8)}
_ORDER = ('qa', 'ka', 'va', 'ga', 'gb', 'qb', 'kb', 'vb', 'qi', 'ki', 'wi')
D_IN = 6984
PROJ_TN = 640
D_PAD = 11 * PROJ_TN
PROJ_TM = 512
_QA_BLK, _KA_BLK, _VA_BLK = 0, 8, 16
_GA_BLK, _GB_BLK, _QB_BLK = 3, 4, 5
_KB_BLK, _VB_BLK, _KI_BLK = 48, 49, 54
_QI_BLK = 10

A_TQ = 256
B_TQ = 128
OUT_TM = 256

_VMEM_LIMIT = 56 * 1024 * 1024


def _nt_dot(a, b):
    return lax.dot_general(a, b, (((1,), (1,)), ((), ())), preferred_element_type=F32)


def _proj_kernel(x_ref, w_ref, b_ref, tab_ref, o_ref, xb_ref):
    j = pl.program_id(1)

    @pl.when(j == 0)
    def _():
        xb_ref[...] = x_ref[...].astype(BF16)

    acc = jnp.dot(xb_ref[...], w_ref[...], preferred_element_type=F32) + b_ref[...]

    def tab(k):
        return tab_ref[:, k * LANES:(k + 1) * LANES]

    def rope_head(t):
        return t * tab(0) + pltpu.roll(t, 64, 1) * tab(1)

    def rope_idx(t, base):
        return (t * tab(base) + pltpu.roll(t, 96, 1) * tab(base + 1)
                + pltpu.roll(t, 32, 1) * tab(base + 2))

    def emit(kinds):
        for k, kind in enumerate(kinds):
            t = acc[:, k * LANES:(k + 1) * LANES]
            if kind == 'h':
                t = rope_head(t)
            elif kind == 'i':
                t = rope_idx(t, 2)
            elif kind == 'l':
                t = rope_idx(t, 5)
            o_ref[:, k * LANES:(k + 1) * LANES] = t.astype(BF16)

    @pl.when(j < 8)
    def _():
        emit('ppppp')

    @pl.when(j == 8)
    def _():
        emit('hhhhh')

    @pl.when(j == 9)
    def _():
        emit('hhhhp')

    @pl.when(j == 10)
    def _():
        emit('iiiil')


def _proj(x2d, wp, bp, tabs):
    m = x2d.shape[0]
    return pl.pallas_call(
        _proj_kernel,
        grid=(m // PROJ_TM, D_PAD // PROJ_TN),
        in_specs=[
            pl.BlockSpec((PROJ_TM, D_MODEL), lambda i, j: (i, 0)),
            pl.BlockSpec((D_MODEL, PROJ_TN), lambda i, j: (0, j)),
            pl.BlockSpec((1, PROJ_TN), lambda i, j: (0, j)),
            pl.BlockSpec((PROJ_TM, 8 * LANES), lambda i, j: (i, 0)),
        ],
        out_specs=pl.BlockSpec((PROJ_TM, PROJ_TN), lambda i, j: (i, j)),
        out_shape=jax.ShapeDtypeStruct((m, D_PAD), BF16),
        scratch_shapes=[pltpu.VMEM((PROJ_TM, D_MODEL), BF16)],
        compiler_params=pltpu.CompilerParams(
            dimension_semantics=("parallel", "arbitrary"), vmem_limit_bytes=_VMEM_LIMIT),
        name="proj",
    )(x2d, wp, bp, tabs)


def _attn_a_kernel(q_ref, k0_ref, k1_ref, k2_ref, v0_ref, v1_ref, v2_ref, bias_ref, o_ref):
    i = pl.program_id(2)
    q = q_ref[0]
    k = jnp.concatenate([k0_ref[0], k1_ref[0], k2_ref[0]], axis=0)
    v = jnp.concatenate([v0_ref[0], v1_ref[0], v2_ref[0]], axis=0)
    s = _nt_dot(q, k) * (HEAD_DIM ** -0.5) + bias_ref[0]
    kpos = (i - 2) * A_TQ + lax.broadcasted_iota(jnp.int32, s.shape, 1)
    s = jnp.where(kpos >= 0, s, NEG)
    m = jnp.max(s, axis=-1, keepdims=True)
    p = jnp.exp(s - m)
    l = jnp.sum(p, axis=-1, keepdims=True)
    o = jnp.dot(p.astype(BF16), v, preferred_element_type=F32) / l
    o_ref[0] = o.astype(BF16)


def _attn_a(pj3, bias_tab):
    b, s, _ = pj3.shape
    nt = s // A_TQ
    blk = (1, A_TQ, HEAD_DIM)

    def kv_spec(col0, back):
        return pl.BlockSpec(blk, lambda h, bb, i: (bb, jnp.maximum(i - back, 0), col0 + h))

    return pl.pallas_call(
        _attn_a_kernel,
        grid=(H_A, b, nt),
        in_specs=[
            pl.BlockSpec(blk, lambda h, bb, i: (bb, i, _QA_BLK + h)),
            kv_spec(_KA_BLK, 2), kv_spec(_KA_BLK, 1), kv_spec(_KA_BLK, 0),
            kv_spec(_VA_BLK, 2), kv_spec(_VA_BLK, 1), kv_spec(_VA_BLK, 0),
            pl.BlockSpec((1, A_TQ, 3 * A_TQ), lambda h, bb, i: (h, 0, 0)),
        ],
        out_specs=pl.BlockSpec(blk, lambda h, bb, i: (bb, i, h)),
        out_shape=jax.ShapeDtypeStruct((b, s, D_A), BF16),
        compiler_params=pltpu.CompilerParams(
            dimension_semantics=("arbitrary", "parallel", "parallel"),
            vmem_limit_bytes=_VMEM_LIMIT),
        name="attn_a",
    )(pj3, pj3, pj3, pj3, pj3, pj3, pj3, bias_tab)


def _float_key(v):
    bits = int(np.float32(v).view(np.int32))
    return bits if bits >= 0 else bits ^ 0x7FFFFFFF


_KEY_LO = _float_key(NEG)
_KEY_HI = 0x7F800001


def _key_to_float(k):
    bits = jnp.where(k >= 0, k, k ^ 0x7FFFFFFF)
    return lax.bitcast_convert_type(bits, F32)


def _topk_mask_bias(score_ref, bias_ref, topk):
    tq, s = score_ref.shape
    nt = s // LANES
    shape = (tq, LANES)

    def count(pred):
        acc = jnp.zeros(shape, F32)
        for t in range(nt):
            acc = acc + jnp.where(pred(score_ref[:, t * LANES:(t + 1) * LANES], t), 1.0, 0.0)
        return jnp.broadcast_to(jnp.sum(acc, axis=1, keepdims=True), shape)

    def bisect(_, carry):
        lo, hi = carry
        mid = lo + lax.shift_right_logical(hi - lo, 1)
        th = _key_to_float(mid)
        ge = count(lambda sc, t: sc >= th) >= topk
        return jnp.where(ge, mid, lo), jnp.where(ge, hi, mid)

    lo, _ = lax.fori_loop(0, 32, bisect,
                          (jnp.full(shape, _KEY_LO, jnp.int32), jnp.full(shape, _KEY_HI, jnp.int32)))
    thr = _key_to_float(lo)
    need = topk - count(lambda sc, t: sc > thr)

    lane = lax.broadcasted_iota(jnp.int32, shape, 1)

    def tie_bisect(b, jsel):
        cand = jsel + jnp.left_shift(jnp.int32(1), 10 - b)
        c = count(lambda sc, t: (sc == thr) & (lane + t * LANES < cand))
        return jnp.where(c < need, cand, jsel)

    jsel = lax.fori_loop(0, 11, tie_bisect, jnp.zeros(shape, jnp.int32))

    for t in range(nt):
        sc = score_ref[:, t * LANES:(t + 1) * LANES]
        sel = (sc > thr) | ((sc == thr) & (lane + t * LANES <= jsel))
        sel = sel & (sc > 0.5 * NEG)
        bias_ref[:, t * LANES:(t + 1) * LANES] = jnp.where(sel, 0.0, NEG)


def _attn_b_kernel(qb_ref, kb_ref, vb_ref, qiw_ref, kiw_ref, o_ref, score_ref, bias_ref, *, topk):
    i = pl.program_id(1)
    tq, s = score_ref.shape
    ki = kiw_ref[0][:, :D_IDX]
    wi = qiw_ref[0, :, H_IDX * D_IDX + D_IDX:H_IDX * D_IDX + D_IDX + H_IDX].astype(F32)
    wi = wi * (H_IDX ** -0.5)

    score = jnp.zeros((tq, s), F32)
    for h in range(H_IDX):
        qh = qiw_ref[0, :, h * D_IDX:(h + 1) * D_IDX]
        dots = _nt_dot(qh, ki) * (D_IDX ** -0.5)
        score = score + wi[:, h:h + 1] * jnp.maximum(dots, 0.0)
    t_pos = i * tq + lax.broadcasted_iota(jnp.int32, (tq, s), 0)
    visible_end = (t_pos // CHUNK + 1) * CHUNK
    key_pos = lax.broadcasted_iota(jnp.int32, (tq, s), 1)
    score_ref[...] = jnp.where(key_pos < visible_end, score, NEG)

    _topk_mask_bias(score_ref, bias_ref, topk)

    kb = kb_ref[0]
    vb = vb_ref[0]
    for h in range(H_B):
        qh = qb_ref[0, :, h * HEAD_DIM:(h + 1) * HEAD_DIM]
        sc = _nt_dot(qh, kb) * (HEAD_DIM ** -0.5) + bias_ref[...]
        m = jnp.max(sc, axis=-1, keepdims=True)
        p = jnp.exp(sc - m)
        l = jnp.sum(p, axis=-1, keepdims=True)
        o = jnp.dot(p.astype(BF16), vb, preferred_element_type=F32) / l
        o_ref[0, :, h * HEAD_DIM:(h + 1) * HEAD_DIM] = o.astype(BF16)


def _attn_b(pj3, topk):
    b, s, _ = pj3.shape
    return pl.pallas_call(
        functools.partial(_attn_b_kernel, topk=topk),
        grid=(b, s // B_TQ),
        in_specs=[
            pl.BlockSpec((1, B_TQ, D_B), lambda bb, i: (bb, i, _QB_BLK)),
            pl.BlockSpec((1, s, HEAD_DIM), lambda bb, i: (bb, 0, _KB_BLK)),
            pl.BlockSpec((1, s, HEAD_DIM), lambda bb, i: (bb, 0, _VB_BLK)),
            pl.BlockSpec((1, B_TQ, PROJ_TN), lambda bb, i: (bb, i, _QI_BLK)),
            pl.BlockSpec((1, s, LANES), lambda bb, i: (bb, 0, _KI_BLK)),
        ],
        out_specs=pl.BlockSpec((1, B_TQ, D_B), lambda bb, i: (bb, i, 0)),
        out_shape=jax.ShapeDtypeStruct((b, s, D_B), BF16),
        scratch_shapes=[pltpu.VMEM((B_TQ, s), F32), pltpu.VMEM((B_TQ, s), F32)],
        compiler_params=pltpu.CompilerParams(
            dimension_semantics=("parallel", "parallel"), vmem_limit_bytes=_VMEM_LIMIT),
        name="attn_b",
    )(pj3, pj3, pj3, pj3, pj3)


def _sigmoid(v):
    return 1.0 / (1.0 + jnp.exp(-v))


def _out_ln_kernel(x_ref, p_ref, ya_ref, yb_ref, ga_ref, gb_ref, wo_ref, wg_ref, bg_ref, wp_ref,
                   lg_ref, lb_ref, o_ref):
    x = x_ref[...]
    ga = ga_ref[...].astype(F32)
    gb = gb_ref[...].astype(F32)
    ua = (ya_ref[...].astype(F32) * (ga * _sigmoid(ga))).astype(BF16)
    ub = (yb_ref[...].astype(F32) * (gb * _sigmoid(gb))).astype(BF16)
    y = (jnp.dot(ua, wo_ref[:D_A, :], preferred_element_type=F32)
         + jnp.dot(ub, wo_ref[D_A:, :], preferred_element_type=F32))
    gate = _sigmoid(jnp.dot(x.astype(BF16), wg_ref[...], preferred_element_type=F32) + bg_ref[...])
    ple = gate * jnp.dot(p_ref[...].astype(BF16), wp_ref[...], preferred_element_type=F32)
    z = ALPHA * x + y + ple
    mu = jnp.mean(z, axis=-1, keepdims=True)
    zc = z - mu
    var = jnp.mean(zc * zc, axis=-1, keepdims=True)
    o_ref[...] = zc * lax.rsqrt(var + LN_EPS) * lg_ref[...] + lb_ref[...]


def _out_ln(x2d, p2d, ya2d, yb2d, pj, wo, wg, bg, wp, lg, lb):
    m = x2d.shape[0]
    row = lambda i: (i, 0)
    fixed = lambda i: (0, 0)
    once = pl.Buffered(1)
    return pl.pallas_call(
        _out_ln_kernel,
        grid=(m // OUT_TM,),
        in_specs=[
            pl.BlockSpec((OUT_TM, D_MODEL), row),
            pl.BlockSpec((OUT_TM, D_PLE), row),
            pl.BlockSpec((OUT_TM, D_A), row),
            pl.BlockSpec((OUT_TM, D_B), row),
            pl.BlockSpec((OUT_TM, D_A), lambda i: (i, _GA_BLK)),
            pl.BlockSpec((OUT_TM, D_B), lambda i: (i, _GB_BLK)),
            pl.BlockSpec((D_MODEL, D_MODEL), fixed, pipeline_mode=once),
            pl.BlockSpec((D_MODEL, D_MODEL), fixed, pipeline_mode=once),
            pl.BlockSpec((1, D_MODEL), fixed),
            pl.BlockSpec((D_PLE, D_MODEL), fixed, pipeline_mode=once),
            pl.BlockSpec((1, D_MODEL), fixed),
            pl.BlockSpec((1, D_MODEL), fixed),
        ],
        out_specs=pl.BlockSpec((OUT_TM, D_MODEL), row),
        out_shape=jax.ShapeDtypeStruct((m, D_MODEL), F32),
        compiler_params=pltpu.CompilerParams(
            dimension_semantics=("parallel",), vmem_limit_bytes=_VMEM_LIMIT),
        name="out_ln",
    )(x2d, p2d, ya2d, yb2d, pj, pj, wo, wg, bg, wp, lg, lb)


def _reorder_pad(w):
    parts = [w[..., _SEG[n][0]:_SEG[n][0] + _SEG[n][1]] for n in _ORDER]
    parts.append(jnp.zeros(w.shape[:-1] + (D_PAD - D_IN,), w.dtype))
    return jnp.concatenate(parts, axis=-1)


def _rope_tables(positions):
    def cos_sin(dim):
        inv = ROPE_THETA ** (-jnp.arange(0, dim, 2, dtype=F32) / dim)
        ang = positions.astype(F32)[..., None] * inv
        ang = jnp.concatenate([ang, ang], -1).reshape(-1, dim)
        return jnp.cos(ang), jnp.sin(ang)

    cos_h, sin_h = cos_sin(HEAD_DIM)
    lane = jnp.arange(LANES)
    sin_h = jnp.where(lane < HEAD_DIM // 2, -sin_h, sin_h)
    cos_i, sin_i = cos_sin(D_IDX)
    cos_i = jnp.concatenate([cos_i, cos_i], -1)
    sin_i = jnp.concatenate([sin_i, sin_i], -1)
    first_half = (lane % D_IDX) < D_IDX // 2
    sa_i = jnp.where(first_half, -sin_i, 0.0)
    sb_i = jnp.where(first_half, 0.0, sin_i)
    is_ki = lane < D_IDX
    cos_l = jnp.where(is_ki, cos_i, 1.0)
    sa_l = jnp.where(is_ki, sa_i, 0.0)
    sb_l = jnp.where(is_ki, sb_i, 0.0)
    return jnp.concatenate([cos_h, sin_h, cos_i, sa_i, sb_i, cos_l, sa_l, sb_l], axis=-1)


def _band_bias_table(rel_bias):
    q = np.arange(A_TQ)[:, None]
    k = np.arange(3 * A_TQ)[None, :]
    dist = LEFT_CHUNKS * CHUNK + q - k
    idx = np.clip(dist, -REL_CLIP, REL_CLIP) + REL_CLIP
    band = (k // CHUNK >= q // CHUNK) & (k // CHUNK <= q // CHUNK + LEFT_CHUNKS)
    return jnp.where(band[None], rel_bias[:, idx], NEG)


def kernel(x, p, positions, w_in, b_in, rel_bias, w_out, w_ple, w_ple_gate, b_ple_gate, ln_g, ln_b):
    b, s, d = x.shape
    assert (d, s % A_TQ, s % B_TQ, (b * s) % PROJ_TM) == (D_MODEL, 0, 0, 0)
    m = b * s
    topk = min(TOPK_MAX, s // 4)
    tabs = _rope_tables(positions)
    x2d = x.reshape(m, d)
    for i in range(DEPTH):
        wp = _reorder_pad(w_in[i]).astype(BF16)
        bp = _reorder_pad(b_in[i])[None, :]
        pj = _proj(x2d, wp, bp, tabs)
        pj3 = pj.reshape(b, s, D_PAD)
        ya = _attn_a(pj3, _band_bias_table(rel_bias[i]))
        yb = _attn_b(pj3, topk)
        x2d = _out_ln(x2d, p[i].reshape(m, D_PLE), ya.reshape(m, D_A), yb.reshape(m, D_B), pj,
                      w_out[i].astype(BF16), w_ple_gate[i].astype(BF16), b_ple_gate[i][None, :],
                      w_ple[i].astype(BF16), ln_g[i][None, :], ln_b[i][None, :])
    return x2d.reshape(b, s, d)
```

```python
import functools

import numpy as np
import jax
import jax.numpy as jnp
from jax import lax
from jax.experimental import pallas as pl
from jax.experimental.pallas import tpu as pltpu

D_MODEL = 2048
DEPTH = 4
CHUNK = 64
LEFT_CHUNKS = 8
HEAD_DIM = 128
D_A = 1024
D_B = 1024
H_A = 8
H_B = 8
REL_CLIP = 128
H_IDX = 8
D_IDX = 64
TOPK_MAX = 256
D_PLE = 256
ROPE_THETA = 10000.0
LN_EPS = 1e-5
NEG = -1e30
ALPHA = (2.0 * DEPTH) ** 0.25

F32 = jnp.float32
BF16 = jnp.bfloat16
LANES = 128

_SEG = {'qa': (0, 1024), 'ka': (1024, 1024), 'va': (2048, 1024), 'ga': (3072, 1024),
        'qb': (4096, 1024), 'kb': (5120, 128), 'vb': (5248, 128), 'gb': (5376, 1024),
        'qi': (6400, 512), 'ki': (6912, 64), 'wi': (6976, 8)}
_ORDER = ('qa', 'ka', 'va', 'ga', 'gb', 'qb', 'kb', 'vb', 'qi', 'ki', 'wi')
D_IN = 6984
PROJ_TN = 640
D_PAD = 11 * PROJ_TN
PROJ_TM = 512
_QA_BLK, _KA_BLK, _VA_BLK = 0, 8, 16
_GA_BLK, _GB_BLK, _QB_BLK = 3, 4, 5
_KB_BLK, _VB_BLK, _KI_BLK = 48, 49, 54
_QI_BLK = 10

A_TQ = 256
B_TQ = 128
B_KSTEP = 512
OUT_TM = 256

_VMEM_LIMIT = 56 * 1024 * 1024


def _nt_dot(a, b):
    return lax.dot_general(a, b, (((1,), (1,)), ((), ())), preferred_element_type=F32)


def _proj_kernel(x_ref, w_ref, b_ref, tab_ref, o_ref, xb_ref):
    j = pl.program_id(1)

    @pl.when(j == 0)
    def _():
        xb_ref[...] = x_ref[...].astype(BF16)

    acc = jnp.dot(xb_ref[...], w_ref[...], preferred_element_type=F32) + b_ref[...]

    def tab(k):
        return tab_ref[:, k * LANES:(k + 1) * LANES]

    def rope_head(t):
        return t * tab(0) + pltpu.roll(t, 64, 1) * tab(1)

    def rope_idx(t, base):
        return (t * tab(base) + pltpu.roll(t, 96, 1) * tab(base + 1)
                + pltpu.roll(t, 32, 1) * tab(base + 2))

    def emit(kinds):
        for k, kind in enumerate(kinds):
            t = acc[:, k * LANES:(k + 1) * LANES]
            if kind == 'h':
                t = rope_head(t)
            elif kind == 'i':
                t = rope_idx(t, 2)
            elif kind == 'l':
                t = rope_idx(t, 5)
            o_ref[:, k * LANES:(k + 1) * LANES] = t.astype(BF16)

    @pl.when(j < 8)
    def _():
        emit('ppppp')

    @pl.when(j == 8)
    def _():
        emit('hhhhh')

    @pl.when(j == 9)
    def _():
        emit('hhhhp')

    @pl.when(j == 10)
    def _():
        emit('iiiil')


def _proj(x2d, wp, bp, tabs):
    m = x2d.shape[0]
    return pl.pallas_call(
        _proj_kernel,
        grid=(m // PROJ_TM, D_PAD // PROJ_TN),
        in_specs=[
            pl.BlockSpec((PROJ_TM, D_MODEL), lambda i, j: (i, 0)),
            pl.BlockSpec((D_MODEL, PROJ_TN), lambda i, j: (0, j)),
            pl.BlockSpec((1, PROJ_TN), lambda i, j: (0, j)),
            pl.BlockSpec((PROJ_TM, 8 * LANES), lambda i, j: (i, 0)),
        ],
        out_specs=pl.BlockSpec((PROJ_TM, PROJ_TN), lambda i, j: (i, j)),
        out_shape=jax.ShapeDtypeStruct((m, D_PAD), BF16),
        scratch_shapes=[pltpu.VMEM((PROJ_TM, D_MODEL), BF16)],
        compiler_params=pltpu.CompilerParams(
            dimension_semantics=("parallel", "arbitrary"), vmem_limit_bytes=_VMEM_LIMIT),
        name="proj",
    )(x2d, wp, bp, tabs)


def _attn_a_kernel(q_ref, k0_ref, k1_ref, k2_ref, v0_ref, v1_ref, v2_ref, bias_ref, o_ref):
    i = pl.program_id(2)
    q = q_ref[0]
    k = jnp.concatenate([k0_ref[0], k1_ref[0], k2_ref[0]], axis=0)
    v = jnp.concatenate([v0_ref[0], v1_ref[0], v2_ref[0]], axis=0)
    s = _nt_dot(q, k) * (HEAD_DIM ** -0.5) + bias_ref[0]
    kpos = (i - 2) * A_TQ + lax.broadcasted_iota(jnp.int32, s.shape, 1)
    s = jnp.where(kpos >= 0, s, NEG)
    m = jnp.max(s, axis=-1, keepdims=True)
    p = jnp.exp(s - m)
    l = jnp.sum(p, axis=-1, keepdims=True)
    o = jnp.dot(p.astype(BF16), v, preferred_element_type=F32) / l
    o_ref[0] = o.astype(BF16)


def _attn_a(pj3, bias_tab):
    b, s, _ = pj3.shape
    nt = s // A_TQ
    blk = (1, A_TQ, HEAD_DIM)

    def kv_spec(col0, back):
        return pl.BlockSpec(blk, lambda h, bb, i: (bb, jnp.maximum(i - back, 0), col0 + h))

    return pl.pallas_call(
        _attn_a_kernel,
        grid=(H_A, b, nt),
        in_specs=[
            pl.BlockSpec(blk, lambda h, bb, i: (bb, i, _QA_BLK + h)),
            kv_spec(_KA_BLK, 2), kv_spec(_KA_BLK, 1), kv_spec(_KA_BLK, 0),
            kv_spec(_VA_BLK, 2), kv_spec(_VA_BLK, 1), kv_spec(_VA_BLK, 0),
            pl.BlockSpec((1, A_TQ, 3 * A_TQ), lambda h, bb, i: (h, 0, 0)),
        ],
        out_specs=pl.BlockSpec(blk, lambda h, bb, i: (bb, i, h)),
        out_shape=jax.ShapeDtypeStruct((b, s, D_A), BF16),
        compiler_params=pltpu.CompilerParams(
            dimension_semantics=("arbitrary", "parallel", "parallel"),
            vmem_limit_bytes=_VMEM_LIMIT),
        name="attn_a",
    )(pj3, pj3, pj3, pj3, pj3, pj3, pj3, bias_tab)


def _float_key(v):
    bits = int(np.float32(v).view(np.int32))
    return bits if bits >= 0 else bits ^ 0x7FFFFFFF


_KEY_LO = _float_key(NEG)
_KEY_HI = 0x7F800001


def _key_to_float(k):
    bits = jnp.where(k >= 0, k, k ^ 0x7FFFFFFF)
    return lax.bitcast_convert_type(bits, F32)


def _topk_mask_bias(score_ref, bias_ref, nk, topk):
    tq = score_ref.shape[0]
    nt = nk // LANES
    shape = (tq, LANES)

    def tile(t):
        return score_ref[:, t * LANES:(t + 1) * LANES]

    def count(pred):
        acc = jnp.zeros(shape, F32)
        for t in range(nt):
            acc = acc + jnp.where(pred(tile(t), t), 1.0, 0.0)
        return jnp.broadcast_to(jnp.sum(acc, axis=1, keepdims=True), shape)

    def bisect(_, carry):
        lo, hi, c_lo = carry
        mid = lo + lax.shift_right_logical(hi - lo, 1)
        th = _key_to_float(mid)
        c = count(lambda sc, t: sc >= th)
        ge = c >= topk
        return jnp.where(ge, mid, lo), jnp.where(ge, hi, mid), jnp.where(ge, c, c_lo)

    lo, _, c_lo = lax.fori_loop(
        0, 32, bisect,
        (jnp.full(shape, _KEY_LO, jnp.int32), jnp.full(shape, _KEY_HI, jnp.int32),
         jnp.full(shape, float(nk), F32)))
    thr = _key_to_float(lo)
    partial = jnp.where((c_lo > topk) & (thr > 0.5 * NEG), 1.0, 0.0)
    any_partial = jnp.max(jnp.max(partial, axis=1, keepdims=True), axis=0, keepdims=True)[0, 0] > 0.0

    @pl.when(jnp.logical_not(any_partial))
    def _():
        for t in range(nt):
            sc = tile(t)
            sel = (sc >= thr) & (sc > 0.5 * NEG)
            bias_ref[:, t * LANES:(t + 1) * LANES] = jnp.where(sel, 0.0, NEG)

    @pl.when(any_partial)
    def _():
        need = topk - count(lambda sc, t: sc > thr)
        lane = lax.broadcasted_iota(jnp.int32, shape, 1)

        def tie_bisect(b, jsel):
            cand = jsel + jnp.left_shift(jnp.int32(1), 10 - b)
            c = count(lambda sc, t: (sc == thr) & (lane + t * LANES < cand))
            return jnp.where(c < need, cand, jsel)

        jsel = lax.fori_loop(0, 11, tie_bisect, jnp.zeros(shape, jnp.int32))
        for t in range(nt):
            sc = tile(t)
            sel = (sc > thr) | ((sc == thr) & (lane + t * LANES <= jsel))
            sel = sel & (sc > 0.5 * NEG)
            bias_ref[:, t * LANES:(t + 1) * LANES] = jnp.where(sel, 0.0, NEG)


def _attn_b_block(nk, i, qb_ref, kb_ref, vb_ref, qiw_ref, kiw_ref, o_ref, score_ref, bias_ref, topk):
    tq = score_ref.shape[0]
    ki = kiw_ref[0, :nk, :D_IDX]
    wi = qiw_ref[0, :, H_IDX * D_IDX + D_IDX:H_IDX * D_IDX + D_IDX + H_IDX].astype(F32)
    wi = wi * (H_IDX ** -0.5)

    score = jnp.zeros((tq, nk), F32)
    for h in range(H_IDX):
        qh = qiw_ref[0, :, h * D_IDX:(h + 1) * D_IDX]
        dots = _nt_dot(qh, ki) * (D_IDX ** -0.5)
        score = score + wi[:, h:h + 1] * jnp.maximum(dots, 0.0)
    t_pos = i * tq + lax.broadcasted_iota(jnp.int32, (tq, nk), 0)
    visible_end = (t_pos // CHUNK + 1) * CHUNK
    key_pos = lax.broadcasted_iota(jnp.int32, (tq, nk), 1)
    score_ref[:, :nk] = jnp.where(key_pos < visible_end, score, NEG)

    _topk_mask_bias(score_ref, bias_ref, nk, topk)

    kb = kb_ref[0, :nk, :]
    vb = vb_ref[0, :nk, :]
    for h in range(H_B):
        qh = qb_ref[0, :, h * HEAD_DIM:(h + 1) * HEAD_DIM]
        sc = _nt_dot(qh, kb) * (HEAD_DIM ** -0.5) + bias_ref[:, :nk]
        m = jnp.max(sc, axis=-1, keepdims=True)
        p = jnp.exp(sc - m)
        l = jnp.sum(p, axis=-1, keepdims=True)
        o = jnp.dot(p.astype(BF16), vb, preferred_element_type=F32) / l
        o_ref[0, :, h * HEAD_DIM:(h + 1) * HEAD_DIM] = o.astype(BF16)


def _attn_b_kernel(qb_ref, kb_ref, vb_ref, qiw_ref, kiw_ref, o_ref, score_ref, bias_ref, *, topk):
    i = pl.program_id(1)
    tq, s = score_ref.shape
    per = B_KSTEP // tq
    for n in range(1, s // B_KSTEP + 1):
        @pl.when(i // per == n - 1)
        def _(n=n):
            _attn_b_block(n * B_KSTEP, i, qb_ref, kb_ref, vb_ref, qiw_ref, kiw_ref, o_ref,
                          score_ref, bias_ref, topk)


def _attn_b(pj3, topk):
    b, s, _ = pj3.shape
    return pl.pallas_call(
        functools.partial(_attn_b_kernel, topk=topk),
        grid=(b, s // B_TQ),
        in_specs=[
            pl.BlockSpec((1, B_TQ, D_B), lambda bb, i: (bb, i, _QB_BLK)),
            pl.BlockSpec((1, s, HEAD_DIM), lambda bb, i: (bb, 0, _KB_BLK)),
            pl.BlockSpec((1, s, HEAD_DIM), lambda bb, i: (bb, 0, _VB_BLK)),
            pl.BlockSpec((1, B_TQ, PROJ_TN), lambda bb, i: (bb, i, _QI_BLK)),
            pl.BlockSpec((1, s, LANES), lambda bb, i: (bb, 0, _KI_BLK)),
        ],
        out_specs=pl.BlockSpec((1, B_TQ, D_B), lambda bb, i: (bb, i, 0)),
        out_shape=jax.ShapeDtypeStruct((b, s, D_B), BF16),
        scratch_shapes=[pltpu.VMEM((B_TQ, s), F32), pltpu.VMEM((B_TQ, s), F32)],
        compiler_params=pltpu.CompilerParams(
            dimension_semantics=("parallel", "parallel"), vmem_limit_bytes=_VMEM_LIMIT),
        name="attn_b",
    )(pj3, pj3, pj3, pj3, pj3)


def _sigmoid(v):
    return 1.0 / (1.0 + jnp.exp(-v))


def _out_ln_kernel(x_ref, p_ref, ya_ref, yb_ref, ga_ref, gb_ref, wo_ref, wg_ref, bg_ref, wp_ref,
                   lg_ref, lb_ref, o_ref):
    x = x_ref[...]
    ga = ga_ref[...].astype(F32)
    gb = gb_ref[...].astype(F32)
    ua = (ya_ref[...].astype(F32) * (ga * _sigmoid(ga))).astype(BF16)
    ub = (yb_ref[...].astype(F32) * (gb * _sigmoid(gb))).astype(BF16)
    y = (jnp.dot(ua, wo_ref[:D_A, :], preferred_element_type=F32)
         + jnp.dot(ub, wo_ref[D_A:, :], preferred_element_type=F32))
    gate = _sigmoid(jnp.dot(x.astype(BF16), wg_ref[...], preferred_element_type=F32) + bg_ref[...])
    ple = gate * jnp.dot(p_ref[...].astype(BF16), wp_ref[...], preferred_element_type=F32)
    z = ALPHA * x + y + ple
    mu = jnp.mean(z, axis=-1, keepdims=True)
    zc = z - mu
    var = jnp.mean(zc * zc, axis=-1, keepdims=True)
    o_ref[...] = zc * lax.rsqrt(var + LN_EPS) * lg_ref[...] + lb_ref[...]


def _out_ln(x2d, p2d, ya2d, yb2d, pj, wo, wg, bg, wp, lg, lb):
    m = x2d.shape[0]
    row = lambda i: (i, 0)
    fixed = lambda i: (0, 0)
    once = pl.Buffered(1)
    return pl.pallas_call(
        _out_ln_kernel,
        grid=(m // OUT_TM,),
        in_specs=[
            pl.BlockSpec((OUT_TM, D_MODEL), row),
            pl.BlockSpec((OUT_TM, D_PLE), row),
            pl.BlockSpec((OUT_TM, D_A), row),
            pl.BlockSpec((OUT_TM, D_B), row),
            pl.BlockSpec((OUT_TM, D_A), lambda i: (i, _GA_BLK)),
            pl.BlockSpec((OUT_TM, D_B), lambda i: (i, _GB_BLK)),
            pl.BlockSpec((D_MODEL, D_MODEL), fixed, pipeline_mode=once),
            pl.BlockSpec((D_MODEL, D_MODEL), fixed, pipeline_mode=once),
            pl.BlockSpec((1, D_MODEL), fixed),
            pl.BlockSpec((D_PLE, D_MODEL), fixed, pipeline_mode=once),
            pl.BlockSpec((1, D_MODEL), fixed),
            pl.BlockSpec((1, D_MODEL), fixed),
        ],
        out_specs=pl.BlockSpec((OUT_TM, D_MODEL), row),
        out_shape=jax.ShapeDtypeStruct((m, D_MODEL), F32),
        compiler_params=pltpu.CompilerParams(
            dimension_semantics=("parallel",), vmem_limit_bytes=_VMEM_LIMIT),
        name="out_ln",
    )(x2d, p2d, ya2d, yb2d, pj, pj, wo, wg, bg, wp, lg, lb)


def _reorder_pad(w):
    parts = [w[..., _SEG[n][0]:_SEG[n][0] + _SEG[n][1]] for n in _ORDER]
    parts.append(jnp.zeros(w.shape[:-1] + (D_PAD - D_IN,), w.dtype))
    return jnp.concatenate(parts, axis=-1)


def _rope_tables(positions):
    def cos_sin(dim):
        inv = ROPE_THETA ** (-jnp.arange(0, dim, 2, dtype=F32) / dim)
        ang = positions.astype(F32)[..., None] * inv
        ang = jnp.concatenate([ang, ang], -1).reshape(-1, dim)
        return jnp.cos(ang), jnp.sin(ang)

    cos_h, sin_h = cos_sin(HEAD_DIM)
    lane = jnp.arange(LANES)
    sin_h = jnp.where(lane < HEAD_DIM // 2, -sin_h, sin_h)
    cos_i, sin_i = cos_sin(D_IDX)
    cos_i = jnp.concatenate([cos_i, cos_i], -1)
    sin_i = jnp.concatenate([sin_i, sin_i], -1)
    first_half = (lane % D_IDX) < D_IDX // 2
    sa_i = jnp.where(first_half, -sin_i, 0.0)
    sb_i = jnp.where(first_half, 0.0, sin_i)
    is_ki = lane < D_IDX
    cos_l = jnp.where(is_ki, cos_i, 1.0)
    sa_l = jnp.where(is_ki, sa_i, 0.0)
    sb_l = jnp.where(is_ki, sb_i, 0.0)
    return jnp.concatenate([cos_h, sin_h, cos_i, sa_i, sb_i, cos_l, sa_l, sb_l], axis=-1)


def _band_bias_table(rel_bias):
    h = rel_bias.shape[0]
    nk = 3 * A_TQ
    n = A_TQ + nk
    n_far = nk - 1 - REL_CLIP
    e = jnp.concatenate([
        jnp.broadcast_to(rel_bias[:, -1:], (h, n_far)),
        rel_bias[:, ::-1],
        jnp.broadcast_to(rel_bias[:, :1], (h, n - n_far - (2 * REL_CLIP + 1)))], axis=1)
    t = jnp.broadcast_to(e[:, None, :], (h, A_TQ, n)).reshape(h, A_TQ * n)
    t = t[:, :A_TQ * (n - 1)].reshape(h, A_TQ, n - 1)[:, :, A_TQ - 1:A_TQ - 1 + nk]
    q = np.arange(A_TQ)[:, None]
    k = np.arange(nk)[None, :]
    band = (k // CHUNK >= q // CHUNK) & (k // CHUNK <= q // CHUNK + LEFT_CHUNKS)
    return jnp.where(band[None], t, NEG)


def kernel(x, p, positions, w_in, b_in, rel_bias, w_out, w_ple, w_ple_gate, b_ple_gate, ln_g, ln_b):
    b, s, d = x.shape
    assert (d, s % A_TQ, s % B_TQ, (b * s) % PROJ_TM) == (D_MODEL, 0, 0, 0)
    m = b * s
    topk = min(TOPK_MAX, s // 4)
    tabs = _rope_tables(positions)
    x2d = x.reshape(m, d)
    for i in range(DEPTH):
        wp = _reorder_pad(w_in[i]).astype(BF16)
        bp = _reorder_pad(b_in[i])[None, :]
        pj = _proj(x2d, wp, bp, tabs)
        pj3 = pj.reshape(b, s, D_PAD)
        ya = _attn_a(pj3, _band_bias_table(rel_bias[i]))
        yb = _attn_b(pj3, topk)
        x2d = _out_ln(x2d, p[i].reshape(m, D_PLE), ya.reshape(m, D_A), yb.reshape(m, D_B), pj,
                      w_out[i].astype(BF16), w_ple_gate[i].astype(BF16), b_ple_gate[i][None, :],
                      w_ple[i].astype(BF16), ln_g[i][None, :], ln_b[i][None, :])
    return x2d.reshape(b, s, d)
```

```python
import functools

import numpy as np
import jax
import jax.numpy as jnp
from jax import lax
from jax.experimental import pallas as pl
from jax.experimental.pallas import tpu as pltpu

D_MODEL = 2048
DEPTH = 4
CHUNK = 64
LEFT_CHUNKS = 8
HEAD_DIM = 128
D_A = 1024
D_B = 1024
H_A = 8
H_B = 8
REL_CLIP = 128
H_IDX = 8
D_IDX = 64
TOPK_MAX = 256
D_PLE = 256
ROPE_THETA = 10000.0
LN_EPS = 1e-5
NEG = -1e30
ALPHA = (2.0 * DEPTH) ** 0.25
LOG2E = 1.4426950408889634

F32 = jnp.float32
BF16 = jnp.bfloat16
LANES = 128

_SEG = {'qa': (0, 1024), 'ka': (1024, 1024), 'va': (2048, 1024), 'ga': (3072, 1024),
        'qb': (4096, 1024), 'kb': (5120, 128), 'vb': (5248, 128), 'gb': (5376, 1024),
        'qi': (6400, 512), 'ki': (6912, 64), 'wi': (6976, 8)}
D_IN = 6984
_COL_SCALE = {'qa': HEAD_DIM ** -0.5 * LOG2E, 'qb': HEAD_DIM ** -0.5 * LOG2E,
              'qi': D_IDX ** -0.5, 'wi': H_IDX ** -0.5}
_P1_ORDER = ('qa', 'ka', 'va', 'ga', 'gb')
D_P1 = 5120
_QA_BLK, _KA_BLK, _VA_BLK, _GA_BLK, _GB_BLK = 0, 1, 2, 3, 4
_P2_ORDER = ('qb', 'kb', 'vb', 'qi', 'ki', 'wi')
D_P2 = 1920
_QB_BLK = 0
_KB_BLK, _VB_BLK, _KI_BLK = 8, 9, 14
_QI_BLK = 2
_WI_OFF = H_IDX * D_IDX + D_IDX
P_SUB = 640
_P2_KINDS = ('hhhhh', 'hhhhp', 'iiiil')

P1_TM, P1_TN = 1024, 1280
P2_TM = 512
A_TQ = 256
B_TQ = 256
B_KSTEP = 512
B_HG = 4
OUT_TM = 256

_VMEM_LIMIT = 56 * 1024 * 1024


def _nt_dot(a, b):
    return lax.dot_general(a, b, (((1,), (1,)), ((), ())), preferred_element_type=F32)


def _proj_plain_kernel(x_ref, w_ref, b_ref, o_ref):
    x = x_ref[...]
    for s in range(P1_TN // P_SUB):
        cols = slice(s * P_SUB, (s + 1) * P_SUB)
        acc = jnp.dot(x, w_ref[:, cols], preferred_element_type=F32) + b_ref[:, cols]
        o_ref[:, cols] = acc.astype(BF16)


def _proj_plain(xb, w1, b1):
    m = xb.shape[0]
    return pl.pallas_call(
        _proj_plain_kernel,
        grid=(m // P1_TM, D_P1 // P1_TN),
        in_specs=[
            pl.BlockSpec((P1_TM, D_MODEL), lambda i, j: (i, 0)),
            pl.BlockSpec((D_MODEL, P1_TN), lambda i, j: (0, j)),
            pl.BlockSpec((1, P1_TN), lambda i, j: (0, j)),
        ],
        out_specs=pl.BlockSpec((P1_TM, P1_TN), lambda i, j: (i, j)),
        out_shape=jax.ShapeDtypeStruct((m, D_P1), BF16),
        compiler_params=pltpu.CompilerParams(
            dimension_semantics=("parallel", "arbitrary"), vmem_limit_bytes=_VMEM_LIMIT),
        name="proj_plain",
    )(xb, w1, b1)


def _proj_rope_kernel(x_ref, w_ref, b_ref, tab_ref, o_ref):
    x = x_ref[...]

    def tab(k):
        return tab_ref[:, k * LANES:(k + 1) * LANES]

    def rope_head(t):
        return t * tab(0) + pltpu.roll(t, 64, 1) * tab(1)

    def rope_idx(t, base):
        return (t * tab(base) + pltpu.roll(t, 96, 1) * tab(base + 1)
                + pltpu.roll(t, 32, 1) * tab(base + 2))

    for s, kinds in enumerate(_P2_KINDS):
        cols = slice(s * P_SUB, (s + 1) * P_SUB)
        acc = jnp.dot(x, w_ref[:, cols], preferred_element_type=F32) + b_ref[:, cols]
        for k, kind in enumerate(kinds):
            t = acc[:, k * LANES:(k + 1) * LANES]
            if kind == 'h':
                t = rope_head(t)
            elif kind == 'i':
                t = rope_idx(t, 2)
            elif kind == 'l':
                t = rope_idx(t, 5)
            c0 = s * P_SUB + k * LANES
            o_ref[:, c0:c0 + LANES] = t.astype(BF16)


def _proj_rope(xb, w2, b2, tabs):
    m = xb.shape[0]
    return pl.pallas_call(
        _proj_rope_kernel,
        grid=(m // P2_TM,),
        in_specs=[
            pl.BlockSpec((P2_TM, D_MODEL), lambda i: (i, 0)),
            pl.BlockSpec((D_MODEL, D_P2), lambda i: (0, 0)),
            pl.BlockSpec((1, D_P2), lambda i: (0, 0)),
            pl.BlockSpec((P2_TM, 8 * LANES), lambda i: (i, 0)),
        ],
        out_specs=pl.BlockSpec((P2_TM, D_P2), lambda i: (i, 0)),
        out_shape=jax.ShapeDtypeStruct((m, D_P2), BF16),
        compiler_params=pltpu.CompilerParams(
            dimension_semantics=("parallel",), vmem_limit_bytes=_VMEM_LIMIT),
        name="proj_rope",
    )(xb, w2, b2, tabs)


def _attn_a_kernel(q_ref, k0_ref, k1_ref, k2_ref, v0_ref, v1_ref, v2_ref, bias_ref, o_ref):
    i = pl.program_id(1)

    def tile(mask_left):
        for h in range(H_A):
            cols = slice(h * HEAD_DIM, (h + 1) * HEAD_DIM)
            q = q_ref[0, :, cols]
            k = jnp.concatenate([k0_ref[0, :, cols], k1_ref[0, :, cols], k2_ref[0, :, cols]], axis=0)
            v = jnp.concatenate([v0_ref[0, :, cols], v1_ref[0, :, cols], v2_ref[0, :, cols]], axis=0)
            s = _nt_dot(q, k) + bias_ref[h]
            if mask_left:
                kpos = (i - 2) * A_TQ + lax.broadcasted_iota(jnp.int32, s.shape, 1)
                s = jnp.where(kpos >= 0, s, NEG)
            m = jnp.max(s, axis=-1, keepdims=True)
            p = jnp.exp2(s - m)
            l = jnp.sum(p, axis=-1, keepdims=True)
            o = jnp.dot(p.astype(BF16), v, preferred_element_type=F32) / l
            o_ref[0, :, cols] = o.astype(BF16)

    @pl.when(i < 2)
    def _():
        tile(True)

    @pl.when(i >= 2)
    def _():
        tile(False)


def _attn_a(pj1, bias_tab):
    b, s, _ = pj1.shape
    blk = (1, A_TQ, D_A)

    def kv_spec(col, back):
        return pl.BlockSpec(blk, lambda bb, i: (bb, jnp.maximum(i - back, 0), col))

    return pl.pallas_call(
        _attn_a_kernel,
        grid=(b, s // A_TQ),
        in_specs=[
            pl.BlockSpec(blk, lambda bb, i: (bb, i, _QA_BLK)),
            kv_spec(_KA_BLK, 2), kv_spec(_KA_BLK, 1), kv_spec(_KA_BLK, 0),
            kv_spec(_VA_BLK, 2), kv_spec(_VA_BLK, 1), kv_spec(_VA_BLK, 0),
            pl.BlockSpec((H_A, A_TQ, 3 * A_TQ), lambda bb, i: (0, 0, 0)),
        ],
        out_specs=pl.BlockSpec(blk, lambda bb, i: (bb, i, 0)),
        out_shape=jax.ShapeDtypeStruct((b, s, D_A), BF16),
        compiler_params=pltpu.CompilerParams(
            dimension_semantics=("parallel", "parallel"), vmem_limit_bytes=_VMEM_LIMIT),
        name="attn_a",
    )(pj1, pj1, pj1, pj1, pj1, pj1, pj1, bias_tab)


def _float_key(v):
    bits = int(np.float32(v).view(np.int32))
    return bits if bits >= 0 else bits ^ 0x7FFFFFFF


_KEY_LO = _float_key(NEG)
_KEY_HI = 0x7F800001


def _key_to_float(k):
    bits = jnp.where(k >= 0, k, k ^ 0x7FFFFFFF)
    return lax.bitcast_convert_type(bits, F32)


def _topk_mask_bias(score_ref, bias_ref, nk, topk):
    tq = score_ref.shape[0]
    nt = nk // LANES
    shape = (tq, LANES)
    row_groups = [slice(g * LANES, (g + 1) * LANES) for g in range(tq // LANES)]

    def tile(rows, t):
        return score_ref[rows, t * LANES:(t + 1) * LANES]

    def count(pred):
        out = []
        for rows in row_groups:
            acc = jnp.zeros((LANES, LANES), F32)
            for t in range(nt):
                acc = acc + jnp.where(pred(tile(rows, t), t, rows), 1.0, 0.0)
            out.append(jnp.broadcast_to(jnp.sum(acc, axis=1, keepdims=True), (LANES, LANES)))
        return jnp.concatenate(out, axis=0)

    def bisect(_, carry):
        lo, hi, c_lo = carry
        mid = lo + lax.shift_right_logical(hi - lo, 1)
        th = _key_to_float(mid)
        c = count(lambda sc, t, rows: sc >= th[rows])
        ge = c >= topk
        return jnp.where(ge, mid, lo), jnp.where(ge, hi, mid), jnp.where(ge, c, c_lo)

    lo, _, c_lo = lax.fori_loop(
        0, 32, bisect,
        (jnp.full(shape, _KEY_LO, jnp.int32), jnp.full(shape, _KEY_HI, jnp.int32),
         jnp.full(shape, float(nk), F32)))
    thr = _key_to_float(lo)
    partial = jnp.where((c_lo > topk) & (thr > 0.5 * NEG), 1.0, 0.0)
    any_partial = jnp.max(jnp.max(partial, axis=1, keepdims=True), axis=0, keepdims=True)[0, 0] > 0.0

    @pl.when(jnp.logical_not(any_partial))
    def _():
        for rows in row_groups:
            for t in range(nt):
                sc = tile(rows, t)
                sel = (sc >= thr[rows]) & (sc > 0.5 * NEG)
                bias_ref[rows, t * LANES:(t + 1) * LANES] = jnp.where(sel, 0.0, NEG)

    @pl.when(any_partial)
    def _():
        need = topk - count(lambda sc, t, rows: sc > thr[rows])
        lane = lax.broadcasted_iota(jnp.int32, (LANES, LANES), 1)

        def tie_bisect(b, jsel):
            cand = jsel + jnp.left_shift(jnp.int32(1), 10 - b)
            c = count(lambda sc, t, rows: (sc == thr[rows]) & (lane + t * LANES < cand[rows]))
            return jnp.where(c < need, cand, jsel)

        jsel = lax.fori_loop(0, 11, tie_bisect, jnp.zeros(shape, jnp.int32))
        for rows in row_groups:
            for t in range(nt):
                sc = tile(rows, t)
                sel = (sc > thr[rows]) | ((sc == thr[rows]) & (lane + t * LANES <= jsel[rows]))
                sel = sel & (sc > 0.5 * NEG)
                bias_ref[rows, t * LANES:(t + 1) * LANES] = jnp.where(sel, 0.0, NEG)


def _attn_b_block(nk, i, qb_ref, kb_ref, vb_ref, qiw_ref, kiw_ref, o_ref, score_ref, bias_ref,
                  p_ref, topk):
    tq = score_ref.shape[0]
    ki = kiw_ref[0, :nk, :D_IDX]
    wi = qiw_ref[0, :, _WI_OFF:_WI_OFF + H_IDX].astype(F32)

    score = jnp.zeros((tq, nk), F32)
    for h in range(H_IDX):
        qh = qiw_ref[0, :, h * D_IDX:(h + 1) * D_IDX]
        score = score + wi[:, h:h + 1] * jnp.maximum(_nt_dot(qh, ki), 0.0)
    t_pos = i * tq + lax.broadcasted_iota(jnp.int32, (tq, nk), 0)
    visible_end = (t_pos // CHUNK + 1) * CHUNK
    key_pos = lax.broadcasted_iota(jnp.int32, (tq, nk), 1)
    score_ref[:, :nk] = jnp.where(key_pos < visible_end, score, NEG)

    _topk_mask_bias(score_ref, bias_ref, nk, topk)

    kb = kb_ref[0, :nk, :]
    vb = vb_ref[0, :nk, :]
    for g in range(H_B // B_HG):
        heads = range(g * B_HG, (g + 1) * B_HG)
        q = jnp.concatenate([qb_ref[0, :, h * HEAD_DIM:(h + 1) * HEAD_DIM] for h in heads], axis=0)
        s = _nt_dot(q, kb)
        norm = []
        for j in range(B_HG):
            rows = slice(j * tq, (j + 1) * tq)
            sc = s[rows] + bias_ref[:, :nk]
            m = jnp.max(sc, axis=-1, keepdims=True)
            p = jnp.exp2(sc - m)
            norm.append(jnp.sum(p, axis=-1, keepdims=True))
            p_ref[rows, :nk] = p.astype(BF16)
        o = jnp.dot(p_ref[:, :nk], vb, preferred_element_type=F32)
        for j, h in enumerate(heads):
            o_ref[0, :, h * HEAD_DIM:(h + 1) * HEAD_DIM] = (
                o[j * tq:(j + 1) * tq] / norm[j]).astype(BF16)


def _attn_b_kernel(qb_ref, kb_ref, vb_ref, qiw_ref, kiw_ref, o_ref, score_ref, bias_ref, p_ref, *,
                   topk):
    i = pl.program_id(1)
    tq, s = score_ref.shape
    per = B_KSTEP // tq
    for n in range(1, s // B_KSTEP + 1):
        @pl.when(i // per == n - 1)
        def _(n=n):
            _attn_b_block(n * B_KSTEP, i, qb_ref, kb_ref, vb_ref, qiw_ref, kiw_ref, o_ref,
                          score_ref, bias_ref, p_ref, topk)


def _attn_b(pj2, topk):
    b, s, _ = pj2.shape
    return pl.pallas_call(
        functools.partial(_attn_b_kernel, topk=topk),
        grid=(b, s // B_TQ),
        in_specs=[
            pl.BlockSpec((1, B_TQ, D_B), lambda bb, i: (bb, i, _QB_BLK)),
            pl.BlockSpec((1, s, HEAD_DIM), lambda bb, i: (bb, 0, _KB_BLK)),
            pl.BlockSpec((1, s, HEAD_DIM), lambda bb, i: (bb, 0, _VB_BLK)),
            pl.BlockSpec((1, B_TQ, P_SUB), lambda bb, i: (bb, i, _QI_BLK)),
            pl.BlockSpec((1, s, LANES), lambda bb, i: (bb, 0, _KI_BLK)),
        ],
        out_specs=pl.BlockSpec((1, B_TQ, D_B), lambda bb, i: (bb, i, 0)),
        out_shape=jax.ShapeDtypeStruct((b, s, D_B), BF16),
        scratch_shapes=[pltpu.VMEM((B_TQ, s), F32), pltpu.VMEM((B_TQ, s), F32),
                        pltpu.VMEM((B_HG * B_TQ, s), BF16)],
        compiler_params=pltpu.CompilerParams(
            dimension_semantics=("parallel", "parallel"), vmem_limit_bytes=_VMEM_LIMIT),
        name="attn_b",
    )(pj2, pj2, pj2, pj2, pj2)


def _sigmoid(v):
    return 1.0 / (1.0 + jnp.exp(-v))


def _out_ln_kernel(x_ref, p_ref, ya_ref, yb_ref, ga_ref, gb_ref, wo_ref, wg_ref, bg_ref, wp_ref,
                   lg_ref, lb_ref, o_ref, ob_ref):
    x = x_ref[...]
    ga = ga_ref[...].astype(F32)
    gb = gb_ref[...].astype(F32)
    ua = (ya_ref[...].astype(F32) * (ga * _sigmoid(ga))).astype(BF16)
    ub = (yb_ref[...].astype(F32) * (gb * _sigmoid(gb))).astype(BF16)
    y = (jnp.dot(ua, wo_ref[:D_A, :], preferred_element_type=F32)
         + jnp.dot(ub, wo_ref[D_A:, :], preferred_element_type=F32))
    gate = _sigmoid(jnp.dot(x.astype(BF16), wg_ref[...], preferred_element_type=F32) + bg_ref[...])
    ple = gate * jnp.dot(p_ref[...].astype(BF16), wp_ref[...], preferred_element_type=F32)
    z = ALPHA * x + y + ple
    mu = jnp.mean(z, axis=-1, keepdims=True)
    zc = z - mu
    var = jnp.mean(zc * zc, axis=-1, keepdims=True)
    out = zc * lax.rsqrt(var + LN_EPS) * lg_ref[...] + lb_ref[...]
    o_ref[...] = out
    ob_ref[...] = out.astype(BF16)


def _out_ln(x2d, p2d, ya2d, yb2d, pj1, wo, wg, bg, wp, lg, lb):
    m = x2d.shape[0]
    row = lambda i: (i, 0)
    fixed = lambda i: (0, 0)
    once = pl.Buffered(1)
    return pl.pallas_call(
        _out_ln_kernel,
        grid=(m // OUT_TM,),
        in_specs=[
            pl.BlockSpec((OUT_TM, D_MODEL), row),
            pl.BlockSpec((OUT_TM, D_PLE), row),
            pl.BlockSpec((OUT_TM, D_A), row),
            pl.BlockSpec((OUT_TM, D_B), row),
            pl.BlockSpec((OUT_TM, D_A), lambda i: (i, _GA_BLK)),
            pl.BlockSpec((OUT_TM, D_B), lambda i: (i, _GB_BLK)),
            pl.BlockSpec((D_MODEL, D_MODEL), fixed, pipeline_mode=once),
            pl.BlockSpec((D_MODEL, D_MODEL), fixed, pipeline_mode=once),
            pl.BlockSpec((1, D_MODEL), fixed),
            pl.BlockSpec((D_PLE, D_MODEL), fixed, pipeline_mode=once),
            pl.BlockSpec((1, D_MODEL), fixed),
            pl.BlockSpec((1, D_MODEL), fixed),
        ],
        out_specs=[pl.BlockSpec((OUT_TM, D_MODEL), row), pl.BlockSpec((OUT_TM, D_MODEL), row)],
        out_shape=[jax.ShapeDtypeStruct((m, D_MODEL), F32), jax.ShapeDtypeStruct((m, D_MODEL), BF16)],
        compiler_params=pltpu.CompilerParams(
            dimension_semantics=("parallel",), vmem_limit_bytes=_VMEM_LIMIT),
        name="out_ln",
    )(x2d, p2d, ya2d, yb2d, pj1, pj1, wo, wg, bg, wp, lg, lb)


def _split_in_proj(w, bvec):
    def seg(a, n):
        part = a[..., _SEG[n][0]:_SEG[n][0] + _SEG[n][1]]
        return part * _COL_SCALE[n] if n in _COL_SCALE else part

    def group(a, order, width):
        parts = [seg(a, n) for n in order]
        pad = width - sum(_SEG[n][1] for n in order)
        if pad:
            parts.append(jnp.zeros(a.shape[:-1] + (pad,), a.dtype))
        return jnp.concatenate(parts, axis=-1)

    return (group(w, _P1_ORDER, D_P1).astype(BF16), group(bvec, _P1_ORDER, D_P1)[None, :],
            group(w, _P2_ORDER, D_P2).astype(BF16), group(bvec, _P2_ORDER, D_P2)[None, :])


def _rope_tables(positions):
    def cos_sin(dim):
        inv = ROPE_THETA ** (-jnp.arange(0, dim, 2, dtype=F32) / dim)
        ang = positions.astype(F32)[..., None] * inv
        ang = jnp.concatenate([ang, ang], -1).reshape(-1, dim)
        return jnp.cos(ang), jnp.sin(ang)

    cos_h, sin_h = cos_sin(HEAD_DIM)
    lane = jnp.arange(LANES)
    sin_h = jnp.where(lane < HEAD_DIM // 2, -sin_h, sin_h)
    cos_i, sin_i = cos_sin(D_IDX)
    cos_i = jnp.concatenate([cos_i, cos_i], -1)
    sin_i = jnp.concatenate([sin_i, sin_i], -1)
    first_half = (lane % D_IDX) < D_IDX // 2
    sa_i = jnp.where(first_half, -sin_i, 0.0)
    sb_i = jnp.where(first_half, 0.0, sin_i)
    is_ki = lane < D_IDX
    cos_l = jnp.where(is_ki, cos_i, 1.0)
    sa_l = jnp.where(is_ki, sa_i, 0.0)
    sb_l = jnp.where(is_ki, sb_i, 0.0)
    return jnp.concatenate([cos_h, sin_h, cos_i, sa_i, sb_i, cos_l, sa_l, sb_l], axis=-1)


def _band_bias_table(rel_bias):
    h = rel_bias.shape[0]
    nk = 3 * A_TQ
    n = A_TQ + nk
    n_far = nk - 1 - REL_CLIP
    e = jnp.concatenate([
        jnp.broadcast_to(rel_bias[:, -1:], (h, n_far)),
        rel_bias[:, ::-1],
        jnp.broadcast_to(rel_bias[:, :1], (h, n - n_far - (2 * REL_CLIP + 1)))], axis=1) * LOG2E
    t = jnp.broadcast_to(e[:, None, :], (h, A_TQ, n)).reshape(h, A_TQ * n)
    t = t[:, :A_TQ * (n - 1)].reshape(h, A_TQ, n - 1)[:, :, A_TQ - 1:A_TQ - 1 + nk]
    q = np.arange(A_TQ)[:, None]
    k = np.arange(nk)[None, :]
    band = (k // CHUNK >= q // CHUNK) & (k // CHUNK <= q // CHUNK + LEFT_CHUNKS)
    return jnp.where(band[None], t, NEG)


def kernel(x, p, positions, w_in, b_in, rel_bias, w_out, w_ple, w_ple_gate, b_ple_gate, ln_g, ln_b):
    b, s, d = x.shape
    m = b * s
    assert d == D_MODEL and s % B_KSTEP == 0 and m % P1_TM == 0
    topk = min(TOPK_MAX, s // 4)
    tabs = _rope_tables(positions)
    x2d = x.reshape(m, d)
    xb = x2d.astype(BF16)
    for i in range(DEPTH):
        w1, b1, w2, b2 = _split_in_proj(w_in[i], b_in[i])
        pj1 = _proj_plain(xb, w1, b1)
        pj2 = _proj_rope(xb, w2, b2, tabs)
        ya = _attn_a(pj1.reshape(b, s, D_P1), _band_bias_table(rel_bias[i]))
        yb = _attn_b(pj2.reshape(b, s, D_P2), topk)
        x2d, xb = _out_ln(x2d, p[i].reshape(m, D_PLE), ya.reshape(m, D_A), yb.reshape(m, D_B), pj1,
                          w_out[i].astype(BF16), w_ple_gate[i].astype(BF16),
                          b_ple_gate[i][None, :], w_ple[i].astype(BF16),
                          ln_g[i][None, :], ln_b[i][None, :])
    return x2d.reshape(b, s, d)
```

```python
import functools

import numpy as np
import jax
import jax.numpy as jnp
from jax import lax
from jax.experimental import pallas as pl
from jax.experimental.pallas import tpu as pltpu

D_MODEL = 2048
DEPTH = 4
CHUNK = 64
LEFT_CHUNKS = 8
HEAD_DIM = 128
D_A = 1024
D_B = 1024
H_A = 8
H_B = 8
REL_CLIP = 128
H_IDX = 8
D_IDX = 64
TOPK_MAX = 256
D_PLE = 256
ROPE_THETA = 10000.0
LN_EPS = 1e-5
NEG = -1e30
ALPHA = (2.0 * DEPTH) ** 0.25
LOG2E = 1.4426950408889634

F32 = jnp.float32
BF16 = jnp.bfloat16
LANES = 128

D_IN = 6984
W_BLK = 256
Q_SCALE = HEAD_DIM ** -0.5 * LOG2E
QI_SCALE = D_IDX ** -0.5
WI_SCALE = H_IDX ** -0.5

D_P1 = 5120
_QA_BLK, _KA_BLK, _VA_BLK, _GA_BLK, _GB_BLK = 0, 1, 2, 3, 4
_P1_SPLIT = 4096 // W_BLK
_P1_SKIP = (5376 - 4096) // W_BLK
_P2_SRC = (16, 17, 18, 19, 20, 25, 26, 27)
D_P2 = 7 * W_BLK + LANES
_P2_TAIL_VALID = D_IN - 27 * W_BLK
_QB_BLK = 0
_KB_BLK, _VB_BLK, _KI_BLK = 8, 9, 14
_QI_BLK = 2
_WI_OFF = H_IDX * D_IDX + D_IDX
P_SUB = 640
_P2_KINDS = ('hhhhh', 'hhhhp', 'iiiil')

P1_TM, P1_TN = 2048, 2 * W_BLK
P2_TM = 512
A_TQ = 256
A_DIAG = 4 * A_TQ
B_TQ = 256
B_KSTEP = 512
OUT_TM = 256

_VMEM_LIMIT = 56 * 1024 * 1024


def _nt_dot(a, b):
    return lax.dot_general(a, b, (((1,), (1,)), ((), ())), preferred_element_type=F32)


def _proj_plain_kernel(x_ref, wa_ref, wb_ref, ba_ref, bb_ref, o_ref):
    j = pl.program_id(1)
    w = jnp.concatenate([wa_ref[...].astype(BF16), wb_ref[...].astype(BF16)], axis=1)
    bias = jnp.concatenate([ba_ref[...], bb_ref[...]], axis=1)
    scale = jnp.where(j < D_A // P1_TN, Q_SCALE, 1.0)
    half = P1_TM // 2
    for r in range(2):
        rows = slice(r * half, (r + 1) * half)
        acc = jnp.dot(x_ref[rows, :], w, preferred_element_type=F32) + bias
        o_ref[rows, :] = (acc * scale).astype(BF16)


def _proj_plain(xb, w_in, b_in3, layer):
    m = xb.shape[0]

    def src(u):
        return u + jnp.where(u >= _P1_SPLIT, _P1_SKIP, 0)

    def wspec(k):
        return pl.BlockSpec((None, D_MODEL, W_BLK), lambda i, j: (layer, 0, src(2 * j + k)))

    def bspec(k):
        return pl.BlockSpec((None, 1, W_BLK), lambda i, j: (layer, 0, src(2 * j + k)))

    return pl.pallas_call(
        _proj_plain_kernel,
        grid=(m // P1_TM, D_P1 // P1_TN),
        in_specs=[pl.BlockSpec((P1_TM, D_MODEL), lambda i, j: (i, 0)),
                  wspec(0), wspec(1), bspec(0), bspec(1)],
        out_specs=pl.BlockSpec((P1_TM, P1_TN), lambda i, j: (i, j)),
        out_shape=jax.ShapeDtypeStruct((m, D_P1), BF16),
        compiler_params=pltpu.CompilerParams(
            dimension_semantics=("parallel", "arbitrary"), vmem_limit_bytes=_VMEM_LIMIT),
        name="proj_plain",
    )(xb, w_in, w_in, b_in3, b_in3)


def _proj_rope_kernel(x_ref, *refs):
    nsrc = len(_P2_SRC)
    w_refs, b_refs = refs[:nsrc], refs[nsrc:2 * nsrc]
    scale_ref, tab_ref, o_ref, wbf_ref, bias_ref = refs[2 * nsrc:]

    @pl.when(pl.program_id(0) == 0)
    def _():
        for n in range(nsrc):
            width = W_BLK if n < nsrc - 1 else LANES
            cols = slice(n * W_BLK, n * W_BLK + width)
            w = w_refs[n][:, :width]
            b = b_refs[n][:, :width]
            if n == nsrc - 1:
                valid = lax.broadcasted_iota(jnp.int32, (1, width), 1) < _P2_TAIL_VALID
                w = jnp.where(valid, w, 0.0)
                b = jnp.where(valid, b, 0.0)
            wbf_ref[:, cols] = (w * scale_ref[:, cols]).astype(BF16)
            bias_ref[:, cols] = b * scale_ref[:, cols]

    x = x_ref[...]

    def tab(k):
        return tab_ref[:, k * LANES:(k + 1) * LANES]

    def rope_head(t):
        return t * tab(0) + pltpu.roll(t, 64, 1) * tab(1)

    def rope_idx(t, base):
        return (t * tab(base) + pltpu.roll(t, 96, 1) * tab(base + 1)
                + pltpu.roll(t, 32, 1) * tab(base + 2))

    for s, kinds in enumerate(_P2_KINDS):
        cols = slice(s * P_SUB, (s + 1) * P_SUB)
        acc = jnp.dot(x, wbf_ref[:, cols], preferred_element_type=F32) + bias_ref[:, cols]
        for k, kind in enumerate(kinds):
            t = acc[:, k * LANES:(k + 1) * LANES]
            if kind == 'h':
                t = rope_head(t)
            elif kind == 'i':
                t = rope_idx(t, 2)
            elif kind == 'l':
                t = rope_idx(t, 5)
            c0 = s * P_SUB + k * LANES
            o_ref[:, c0:c0 + LANES] = t.astype(BF16)


def _proj_rope(xb, w_in, b_in3, col_scale, tabs, layer):
    m = xb.shape[0]
    once = pl.Buffered(1)
    wspecs = [pl.BlockSpec((None, D_MODEL, W_BLK), lambda i, u=u: (layer, 0, u), pipeline_mode=once)
              for u in _P2_SRC]
    bspecs = [pl.BlockSpec((None, 1, W_BLK), lambda i, u=u: (layer, 0, u)) for u in _P2_SRC]
    nsrc = len(_P2_SRC)
    return pl.pallas_call(
        _proj_rope_kernel,
        grid=(m // P2_TM,),
        in_specs=[pl.BlockSpec((P2_TM, D_MODEL), lambda i: (i, 0))] + wspecs + bspecs + [
            pl.BlockSpec((1, D_P2), lambda i: (0, 0)),
            pl.BlockSpec((P2_TM, 8 * LANES), lambda i: (i, 0)),
        ],
        out_specs=pl.BlockSpec((P2_TM, D_P2), lambda i: (i, 0)),
        out_shape=jax.ShapeDtypeStruct((m, D_P2), BF16),
        scratch_shapes=[pltpu.VMEM((D_MODEL, D_P2), BF16), pltpu.VMEM((1, D_P2), F32)],
        compiler_params=pltpu.CompilerParams(
            dimension_semantics=("arbitrary",), vmem_limit_bytes=_VMEM_LIMIT),
        name="proj_rope",
    )(xb, *([w_in] * nsrc), *([b_in3] * nsrc), col_scale, tabs)


def _attn_a_kernel(q_ref, k0_ref, k1_ref, k2_ref, v0_ref, v1_ref, v2_ref, diag_ref, o_ref, bias_ref):
    i = pl.program_id(1)
    nk = 3 * A_TQ

    @pl.when((pl.program_id(0) == 0) & (i == 0))
    def _():
        qc = lax.shift_right_logical(lax.broadcasted_iota(jnp.int32, (A_TQ, nk), 0), 6)
        kc = lax.shift_right_logical(lax.broadcasted_iota(jnp.int32, (A_TQ, nk), 1), 6)
        band = (kc >= qc) & (kc <= qc + LEFT_CHUNKS)
        for h in range(H_A):
            rows = jnp.broadcast_to(diag_ref[h:h + 1, :], (A_TQ, A_DIAG))
            t = pltpu.roll(rows, A_DIAG - (A_TQ - 1), 1, stride=1, stride_axis=0)
            bias_ref[h] = jnp.where(band, t[:, :nk], NEG)

    def tile(mask_left):
        for h in range(H_A):
            cols = slice(h * HEAD_DIM, (h + 1) * HEAD_DIM)
            q = q_ref[0, :, cols]
            k = jnp.concatenate([k0_ref[0, :, cols], k1_ref[0, :, cols], k2_ref[0, :, cols]], axis=0)
            v = jnp.concatenate([v0_ref[0, :, cols], v1_ref[0, :, cols], v2_ref[0, :, cols]], axis=0)
            s = _nt_dot(q, k) + bias_ref[h]
            if mask_left:
                kpos = (i - 2) * A_TQ + lax.broadcasted_iota(jnp.int32, s.shape, 1)
                s = jnp.where(kpos >= 0, s, NEG)
            m = jnp.max(s, axis=-1, keepdims=True)
            p = jnp.exp2(s - m)
            l = jnp.sum(p, axis=-1, keepdims=True)
            o = jnp.dot(p.astype(BF16), v, preferred_element_type=F32) / l
            o_ref[0, :, cols] = o.astype(BF16)

    @pl.when(i < 2)
    def _():
        tile(True)

    @pl.when(i >= 2)
    def _():
        tile(False)


def _attn_a(pj1, diag, layer):
    b, s, _ = pj1.shape
    blk = (1, A_TQ, D_A)

    def kv_spec(col, back):
        return pl.BlockSpec(blk, lambda bb, i: (bb, jnp.maximum(i - back, 0), col))

    return pl.pallas_call(
        _attn_a_kernel,
        grid=(b, s // A_TQ),
        in_specs=[
            pl.BlockSpec(blk, lambda bb, i: (bb, i, _QA_BLK)),
            kv_spec(_KA_BLK, 2), kv_spec(_KA_BLK, 1), kv_spec(_KA_BLK, 0),
            kv_spec(_VA_BLK, 2), kv_spec(_VA_BLK, 1), kv_spec(_VA_BLK, 0),
            pl.BlockSpec((None, H_A, A_DIAG), lambda bb, i: (layer, 0, 0)),
        ],
        out_specs=pl.BlockSpec(blk, lambda bb, i: (bb, i, 0)),
        out_shape=jax.ShapeDtypeStruct((b, s, D_A), BF16),
        scratch_shapes=[pltpu.VMEM((H_A, A_TQ, 3 * A_TQ), F32)],
        compiler_params=pltpu.CompilerParams(
            dimension_semantics=("arbitrary", "arbitrary"), vmem_limit_bytes=_VMEM_LIMIT),
        name="attn_a",
    )(pj1, pj1, pj1, pj1, pj1, pj1, pj1, diag)


def _float_key(v):
    bits = int(np.float32(v).view(np.int32))
    return bits if bits >= 0 else bits ^ 0x7FFFFFFF


_KEY_LO = _float_key(NEG)
_KEY_HI = 0x7F800001


def _key_to_float(k):
    bits = jnp.where(k >= 0, k, k ^ 0x7FFFFFFF)
    return lax.bitcast_convert_type(bits, F32)


def _topk_mask_bias(score_ref, bias_ref, nk, topk):
    tq = score_ref.shape[0]
    nt = nk // LANES
    shape = (tq, LANES)
    row_groups = [slice(g * LANES, (g + 1) * LANES) for g in range(tq // LANES)]

    def tile(rows, t):
        return score_ref[rows, t * LANES:(t + 1) * LANES]

    def count(pred):
        out = []
        for rows in row_groups:
            acc = jnp.zeros((LANES, LANES), F32)
            for t in range(nt):
                acc = acc + jnp.where(pred(tile(rows, t), t, rows), 1.0, 0.0)
            out.append(jnp.broadcast_to(jnp.sum(acc, axis=1, keepdims=True), (LANES, LANES)))
        return jnp.concatenate(out, axis=0)

    def bisect(_, carry):
        lo, hi, c_lo = carry
        mid = lo + lax.shift_right_logical(hi - lo, 1)
        th = _key_to_float(mid)
        c = count(lambda sc, t, rows: sc >= th[rows])
        ge = c >= topk
        return jnp.where(ge, mid, lo), jnp.where(ge, hi, mid), jnp.where(ge, c, c_lo)

    lo, _, c_lo = lax.fori_loop(
        0, 32, bisect,
        (jnp.full(shape, _KEY_LO, jnp.int32), jnp.full(shape, _KEY_HI, jnp.int32),
         jnp.full(shape, float(nk), F32)))
    thr = _key_to_float(lo)
    partial = jnp.where((c_lo > topk) & (thr > 0.5 * NEG), 1.0, 0.0)
    any_partial = jnp.max(jnp.max(partial, axis=1, keepdims=True), axis=0, keepdims=True)[0, 0] > 0.0

    @pl.when(jnp.logical_not(any_partial))
    def _():
        for rows in row_groups:
            for t in range(nt):
                sc = tile(rows, t)
                sel = (sc >= thr[rows]) & (sc > 0.5 * NEG)
                bias_ref[rows, t * LANES:(t + 1) * LANES] = jnp.where(sel, 0.0, NEG)

    @pl.when(any_partial)
    def _():
        need = topk - count(lambda sc, t, rows: sc > thr[rows])
        lane = lax.broadcasted_iota(jnp.int32, (LANES, LANES), 1)

        def tie_bisect(b, jsel):
            cand = jsel + jnp.left_shift(jnp.int32(1), 10 - b)
            c = count(lambda sc, t, rows: (sc == thr[rows]) & (lane + t * LANES < cand[rows]))
            return jnp.where(c < need, cand, jsel)

        jsel = lax.fori_loop(0, 11, tie_bisect, jnp.zeros(shape, jnp.int32))
        for rows in row_groups:
            for t in range(nt):
                sc = tile(rows, t)
                sel = (sc > thr[rows]) | ((sc == thr[rows]) & (lane + t * LANES <= jsel[rows]))
                sel = sel & (sc > 0.5 * NEG)
                bias_ref[rows, t * LANES:(t + 1) * LANES] = jnp.where(sel, 0.0, NEG)


def _attn_b_block(nk, i, qb_ref, kb_ref, vb_ref, qiw_ref, kiw_ref, o_ref, score_ref, bias_ref,
                  s_ref, topk):
    tq = score_ref.shape[0]
    ki = kiw_ref[0, :nk, :D_IDX]
    kb = kb_ref[0, :nk, :]
    vb = vb_ref[0, :nk, :]
    wi = qiw_ref[0, :, _WI_OFF:_WI_OFF + H_IDX].astype(F32)

    score = jnp.zeros((tq, nk), F32)
    for h in range(H_IDX):
        qh = qiw_ref[0, :, h * D_IDX:(h + 1) * D_IDX]
        score = score + wi[:, h:h + 1] * jnp.maximum(_nt_dot(qh, ki), 0.0)
    t_pos = i * tq + lax.broadcasted_iota(jnp.int32, (tq, nk), 0)
    visible_end = (t_pos // CHUNK + 1) * CHUNK
    key_pos = lax.broadcasted_iota(jnp.int32, (tq, nk), 1)
    score_ref[:, :nk] = jnp.where(key_pos < visible_end, score, NEG)
    for h in range(H_B):
        s_ref[h, :, :nk] = _nt_dot(qb_ref[0, :, h * HEAD_DIM:(h + 1) * HEAD_DIM], kb)

    _topk_mask_bias(score_ref, bias_ref, nk, topk)

    for h in range(H_B):
        sc = s_ref[h, :, :nk] + bias_ref[:, :nk]
        m = jnp.max(sc, axis=-1, keepdims=True)
        p = jnp.exp2(sc - m)
        l = jnp.sum(p, axis=-1, keepdims=True)
        o = jnp.dot(p.astype(BF16), vb, preferred_element_type=F32) / l
        o_ref[0, :, h * HEAD_DIM:(h + 1) * HEAD_DIM] = o.astype(BF16)


def _attn_b_kernel(qb_ref, kb_ref, vb_ref, qiw_ref, kiw_ref, o_ref, score_ref, bias_ref, s_ref, *,
                   topk):
    i = pl.program_id(1)
    tq, s = score_ref.shape
    per = B_KSTEP // tq
    for n in range(1, s // B_KSTEP + 1):
        @pl.when(i // per == n - 1)
        def _(n=n):
            _attn_b_block(n * B_KSTEP, i, qb_ref, kb_ref, vb_ref, qiw_ref, kiw_ref, o_ref,
                          score_ref, bias_ref, s_ref, topk)


def _attn_b(pj2, topk):
    b, s, _ = pj2.shape
    return pl.pallas_call(
        functools.partial(_attn_b_kernel, topk=topk),
        grid=(b, s // B_TQ),
        in_specs=[
            pl.BlockSpec((1, B_TQ, D_B), lambda bb, i: (bb, i, _QB_BLK)),
            pl.BlockSpec((1, s, HEAD_DIM), lambda bb, i: (bb, 0, _KB_BLK)),
            pl.BlockSpec((1, s, HEAD_DIM), lambda bb, i: (bb, 0, _VB_BLK)),
            pl.BlockSpec((1, B_TQ, P_SUB), lambda bb, i: (bb, i, _QI_BLK)),
            pl.BlockSpec((1, s, LANES), lambda bb, i: (bb, 0, _KI_BLK)),
        ],
        out_specs=pl.BlockSpec((1, B_TQ, D_B), lambda bb, i: (bb, i, 0)),
        out_shape=jax.ShapeDtypeStruct((b, s, D_B), BF16),
        scratch_shapes=[pltpu.VMEM((B_TQ, s), F32), pltpu.VMEM((B_TQ, s), F32),
                        pltpu.VMEM((H_B, B_TQ, s), F32)],
        compiler_params=pltpu.CompilerParams(
            dimension_semantics=("parallel", "parallel"), vmem_limit_bytes=_VMEM_LIMIT),
        name="attn_b",
    )(pj2, pj2, pj2, pj2, pj2)


def _sigmoid(v):
    return 1.0 / (1.0 + jnp.exp(-v))


def _out_ln_kernel(x_ref, p_ref, ya_ref, yb_ref, ga_ref, gb_ref, wo_ref, wg_ref, bg_ref, wp_ref,
                   lg_ref, lb_ref, o_ref, ob_ref):
    x = x_ref[...]
    ga = ga_ref[...].astype(F32)
    gb = gb_ref[...].astype(F32)
    ua = (ya_ref[...].astype(F32) * (ga * _sigmoid(ga))).astype(BF16)
    ub = (yb_ref[...].astype(F32) * (gb * _sigmoid(gb))).astype(BF16)
    y = (jnp.dot(ua, wo_ref[:D_A, :], preferred_element_type=F32)
         + jnp.dot(ub, wo_ref[D_A:, :], preferred_element_type=F32))
    gate = _sigmoid(jnp.dot(x.astype(BF16), wg_ref[...], preferred_element_type=F32) + bg_ref[...])
    ple = gate * jnp.dot(p_ref[...].astype(BF16), wp_ref[...], preferred_element_type=F32)
    z = ALPHA * x + y + ple
    mu = jnp.mean(z, axis=-1, keepdims=True)
    zc = z - mu
    var = jnp.mean(zc * zc, axis=-1, keepdims=True)
    out = zc * lax.rsqrt(var + LN_EPS) * lg_ref[...] + lb_ref[...]
    o_ref[...] = out
    ob_ref[...] = out.astype(BF16)


def _out_ln(x2d, p3, ya2d, yb2d, pj1, wo, wg, bg3, wp, lg3, lb3, layer):
    m = x2d.shape[0]
    row = lambda i: (i, 0)
    fixed = lambda i: (layer, 0, 0)
    once = pl.Buffered(1)
    return pl.pallas_call(
        _out_ln_kernel,
        grid=(m // OUT_TM,),
        in_specs=[
            pl.BlockSpec((OUT_TM, D_MODEL), row),
            pl.BlockSpec((None, OUT_TM, D_PLE), lambda i: (layer, i, 0)),
            pl.BlockSpec((OUT_TM, D_A), row),
            pl.BlockSpec((OUT_TM, D_B), row),
            pl.BlockSpec((OUT_TM, D_A), lambda i: (i, _GA_BLK)),
            pl.BlockSpec((OUT_TM, D_B), lambda i: (i, _GB_BLK)),
            pl.BlockSpec((None, D_MODEL, D_MODEL), fixed, pipeline_mode=once),
            pl.BlockSpec((None, D_MODEL, D_MODEL), fixed, pipeline_mode=once),
            pl.BlockSpec((None, 1, D_MODEL), fixed),
            pl.BlockSpec((None, D_PLE, D_MODEL), fixed, pipeline_mode=once),
            pl.BlockSpec((None, 1, D_MODEL), fixed),
            pl.BlockSpec((None, 1, D_MODEL), fixed),
        ],
        out_specs=[pl.BlockSpec((OUT_TM, D_MODEL), row), pl.BlockSpec((OUT_TM, D_MODEL), row)],
        out_shape=[jax.ShapeDtypeStruct((m, D_MODEL), F32), jax.ShapeDtypeStruct((m, D_MODEL), BF16)],
        compiler_params=pltpu.CompilerParams(
            dimension_semantics=("parallel",), vmem_limit_bytes=_VMEM_LIMIT),
        name="out_ln",
    )(x2d, p3, ya2d, yb2d, pj1, pj1, wo, wg, bg3, wp, lg3, lb3)


def _rope_col_scale():
    sc = np.ones((1, D_P2), np.float32)
    sc[:, :D_B] = Q_SCALE
    qi0 = D_B + 2 * HEAD_DIM
    sc[:, qi0:qi0 + H_IDX * D_IDX] = QI_SCALE
    wi0 = qi0 + H_IDX * D_IDX + D_IDX
    sc[:, wi0:wi0 + H_IDX] = WI_SCALE
    return jnp.asarray(sc)


def _rope_tables(positions):
    def cos_sin(dim):
        inv = ROPE_THETA ** (-jnp.arange(0, dim, 2, dtype=F32) / dim)
        ang = positions.astype(F32)[..., None] * inv
        ang = jnp.concatenate([ang, ang], -1).reshape(-1, dim)
        return jnp.cos(ang), jnp.sin(ang)

    cos_h, sin_h = cos_sin(HEAD_DIM)
    lane = jnp.arange(LANES)
    sin_h = jnp.where(lane < HEAD_DIM // 2, -sin_h, sin_h)
    cos_i, sin_i = cos_sin(D_IDX)
    cos_i = jnp.concatenate([cos_i, cos_i], -1)
    sin_i = jnp.concatenate([sin_i, sin_i], -1)
    first_half = (lane % D_IDX) < D_IDX // 2
    sa_i = jnp.where(first_half, -sin_i, 0.0)
    sb_i = jnp.where(first_half, 0.0, sin_i)
    is_ki = lane < D_IDX
    cos_l = jnp.where(is_ki, cos_i, 1.0)
    sa_l = jnp.where(is_ki, sa_i, 0.0)
    sb_l = jnp.where(is_ki, sb_i, 0.0)
    return jnp.concatenate([cos_h, sin_h, cos_i, sa_i, sb_i, cos_l, sa_l, sb_l], axis=-1)


def _band_bias_diagonals(rel_bias):
    lead = rel_bias.shape[:-1]
    n_far = 3 * A_TQ - 1 - REL_CLIP
    n_near = A_DIAG - n_far - (2 * REL_CLIP + 1)
    return jnp.concatenate([
        jnp.broadcast_to(rel_bias[..., -1:], lead + (n_far,)),
        rel_bias[..., ::-1],
        jnp.broadcast_to(rel_bias[..., :1], lead + (n_near,))], axis=-1) * LOG2E


def kernel(x, p, positions, w_in, b_in, rel_bias, w_out, w_ple, w_ple_gate, b_ple_gate, ln_g, ln_b):
    b, s, d = x.shape
    m = b * s
    assert d == D_MODEL and s % B_KSTEP == 0 and m % P1_TM == 0
    topk = min(TOPK_MAX, s // 4)
    tabs = _rope_tables(positions)
    col_scale = _rope_col_scale()
    diag = _band_bias_diagonals(rel_bias)
    b_in3 = b_in[:, None, :]
    wo, wg, wp = w_out.astype(BF16), w_ple_gate.astype(BF16), w_ple.astype(BF16)
    bg3, lg3, lb3 = b_ple_gate[:, None, :], ln_g[:, None, :], ln_b[:, None, :]
    p3 = p.reshape(DEPTH, m, D_PLE)
    x2d = x.reshape(m, d)
    xb = x2d.astype(BF16)
    for i in range(DEPTH):
        pj1 = _proj_plain(xb, w_in, b_in3, i)
        pj2 = _proj_rope(xb, w_in, b_in3, col_scale, tabs, i)
        ya = _attn_a(pj1.reshape(b, s, D_P1), diag, i)
        yb = _attn_b(pj2.reshape(b, s, D_P2), topk)
        x2d, xb = _out_ln(x2d, p3, ya.reshape(m, D_A), yb.reshape(m, D_B), pj1,
                          wo, wg, bg3, wp, lg3, lb3, i)
    return x2d.reshape(b, s, d)
```

```python
import functools

import numpy as np
import jax
import jax.numpy as jnp
from jax import lax
from jax.experimental import pallas as pl
from jax.experimental.pallas import tpu as pltpu

D_MODEL = 2048
DEPTH = 4
CHUNK = 64
LEFT_CHUNKS = 8
HEAD_DIM = 128
D_A = 1024
D_B = 1024
H_A = 8
H_B = 8
REL_CLIP = 128
H_IDX = 8
D_IDX = 64
TOPK_MAX = 256
D_PLE = 256
ROPE_THETA = 10000.0
LN_EPS = 1e-5
NEG = -1e30
ALPHA = (2.0 * DEPTH) ** 0.25
LOG2E = 1.4426950408889634

F32 = jnp.float32
BF16 = jnp.bfloat16
LANES = 128

D_IN = 6984
W_BLK = 256
Q_SCALE = HEAD_DIM ** -0.5 * LOG2E
QI_SCALE = D_IDX ** -0.5
WI_SCALE = H_IDX ** -0.5

D_P1 = 5120
_QA_BLK, _KA_BLK, _VA_BLK, _GA_BLK, _GB_BLK = 0, 1, 2, 3, 4
_P1_SPLIT = 4096 // W_BLK
_P1_SKIP = (5376 - 4096) // W_BLK
_P2_SRC = (16, 17, 18, 19, 20, 25, 26, 27)
D_P2 = 7 * W_BLK + LANES
_P2_TAIL_VALID = D_IN - 27 * W_BLK
_QB_BLK = 0
_KB_BLK, _VB_BLK, _KI_BLK = 8, 9, 14
_QI_BLK = 2
_WI_OFF = H_IDX * D_IDX + D_IDX
P_SUB = 640
_P2_KINDS = ('hhhhh', 'hhhhp', 'iiiil')

P1_TM, P1_TN = 2048, 2 * W_BLK
P2_TM = 512
A_TQ = 256
A_DIAG = 4 * A_TQ
B_TQ = 256
B_KSTEP = 512
_SEL_ROUNDS, _SEL_PASSES = 4, 5
OUT_TM = 256

_VMEM_LIMIT = 56 * 1024 * 1024


def _nt_dot(a, b):
    return lax.dot_general(a, b, (((1,), (1,)), ((), ())), preferred_element_type=F32)


def _proj_plain_kernel(x_ref, wa_ref, wb_ref, ba_ref, bb_ref, o_ref):
    j = pl.program_id(1)
    w = jnp.concatenate([wa_ref[...].astype(BF16), wb_ref[...].astype(BF16)], axis=0)
    bias = jnp.concatenate([ba_ref[...], bb_ref[...]], axis=1)
    scale = jnp.where(j < D_A // P1_TN, Q_SCALE, 1.0)
    half = P1_TM // 2
    for r in range(2):
        rows = slice(r * half, (r + 1) * half)
        acc = _nt_dot(x_ref[rows, :], w) + bias
        o_ref[rows, :] = (acc * scale).astype(BF16)


def _proj_plain(xb, w_in_t, b_in3, layer):
    m = xb.shape[0]

    def src(u):
        return u + jnp.where(u >= _P1_SPLIT, _P1_SKIP, 0)

    def wspec(k):
        return pl.BlockSpec((None, W_BLK, D_MODEL), lambda i, j: (layer, src(2 * j + k), 0))

    def bspec(k):
        return pl.BlockSpec((None, 1, W_BLK), lambda i, j: (layer, 0, src(2 * j + k)))

    return pl.pallas_call(
        _proj_plain_kernel,
        grid=(m // P1_TM, D_P1 // P1_TN),
        in_specs=[pl.BlockSpec((P1_TM, D_MODEL), lambda i, j: (i, 0)),
                  wspec(0), wspec(1), bspec(0), bspec(1)],
        out_specs=pl.BlockSpec((P1_TM, P1_TN), lambda i, j: (i, j)),
        out_shape=jax.ShapeDtypeStruct((m, D_P1), BF16),
        compiler_params=pltpu.CompilerParams(
            dimension_semantics=("parallel", "arbitrary"), vmem_limit_bytes=_VMEM_LIMIT),
        name="proj_plain",
    )(xb, w_in_t, w_in_t, b_in3, b_in3)


def _proj_rope_kernel(x_ref, *refs):
    nsrc = len(_P2_SRC)
    w_refs, b_refs = refs[:nsrc], refs[nsrc:2 * nsrc]
    scale_ref, tab_ref, o_ref, wbf_ref, bias_ref = refs[2 * nsrc:]

    @pl.when(pl.program_id(0) == 0)
    def _():
        for n in range(nsrc):
            width = W_BLK if n < nsrc - 1 else LANES
            cols = slice(n * W_BLK, n * W_BLK + width)
            w = w_refs[n][:width, :]
            b = b_refs[n][:, :width]
            if n == nsrc - 1:
                w = jnp.where(lax.broadcasted_iota(jnp.int32, (width, 1), 0) < _P2_TAIL_VALID, w, 0.0)
                b = jnp.where(lax.broadcasted_iota(jnp.int32, (1, width), 1) < _P2_TAIL_VALID, b, 0.0)
            wbf_ref[cols, :] = w.astype(BF16)
            bias_ref[:, cols] = b * scale_ref[:, cols]

    x = x_ref[...]

    def tab(k):
        return tab_ref[:, k * LANES:(k + 1) * LANES]

    def rope_head(t):
        return t * tab(0) + pltpu.roll(t, 64, 1) * tab(1)

    def rope_idx(t, base):
        return (t * tab(base) + pltpu.roll(t, 96, 1) * tab(base + 1)
                + pltpu.roll(t, 32, 1) * tab(base + 2))

    for s, kinds in enumerate(_P2_KINDS):
        cols = slice(s * P_SUB, (s + 1) * P_SUB)
        acc = _nt_dot(x, wbf_ref[cols, :]) * scale_ref[:, cols] + bias_ref[:, cols]
        for k, kind in enumerate(kinds):
            t = acc[:, k * LANES:(k + 1) * LANES]
            if kind == 'h':
                t = rope_head(t)
            elif kind == 'i':
                t = rope_idx(t, 2)
            elif kind == 'l':
                t = rope_idx(t, 5)
            c0 = s * P_SUB + k * LANES
            o_ref[:, c0:c0 + LANES] = t.astype(BF16)


def _proj_rope(xb, w_in_t, b_in3, col_scale, tabs, layer):
    m = xb.shape[0]
    once = pl.Buffered(1)
    wspecs = [pl.BlockSpec((None, W_BLK, D_MODEL), lambda i, u=u: (layer, u, 0), pipeline_mode=once)
              for u in _P2_SRC]
    bspecs = [pl.BlockSpec((None, 1, W_BLK), lambda i, u=u: (layer, 0, u)) for u in _P2_SRC]
    nsrc = len(_P2_SRC)
    return pl.pallas_call(
        _proj_rope_kernel,
        grid=(m // P2_TM,),
        in_specs=[pl.BlockSpec((P2_TM, D_MODEL), lambda i: (i, 0))] + wspecs + bspecs + [
            pl.BlockSpec((1, D_P2), lambda i: (0, 0)),
            pl.BlockSpec((P2_TM, 8 * LANES), lambda i: (i, 0)),
        ],
        out_specs=pl.BlockSpec((P2_TM, D_P2), lambda i: (i, 0)),
        out_shape=jax.ShapeDtypeStruct((m, D_P2), BF16),
        scratch_shapes=[pltpu.VMEM((D_P2, D_MODEL), BF16), pltpu.VMEM((1, D_P2), F32)],
        compiler_params=pltpu.CompilerParams(
            dimension_semantics=("arbitrary",), vmem_limit_bytes=_VMEM_LIMIT),
        name="proj_rope",
    )(xb, *([w_in_t] * nsrc), *([b_in3] * nsrc), col_scale, tabs)


def _attn_a_kernel(q_ref, k0_ref, k1_ref, k2_ref, v0_ref, v1_ref, v2_ref, diag_ref, o_ref, bias_ref):
    i = pl.program_id(1)
    nk = 3 * A_TQ

    @pl.when((pl.program_id(0) == 0) & (i == 0))
    def _():
        qc = lax.shift_right_logical(lax.broadcasted_iota(jnp.int32, (A_TQ, nk), 0), 6)
        kc = lax.shift_right_logical(lax.broadcasted_iota(jnp.int32, (A_TQ, nk), 1), 6)
        band = (kc >= qc) & (kc <= qc + LEFT_CHUNKS)
        for h in range(H_A):
            rows = jnp.broadcast_to(diag_ref[h:h + 1, :], (A_TQ, A_DIAG))
            t = pltpu.roll(rows, A_DIAG - (A_TQ - 1), 1, stride=1, stride_axis=0)
            bias_ref[h] = jnp.where(band, t[:, :nk], NEG)

    def tile(mask_left):
        for h in range(H_A):
            cols = slice(h * HEAD_DIM, (h + 1) * HEAD_DIM)
            q = q_ref[0, :, cols]
            k = jnp.concatenate([k0_ref[0, :, cols], k1_ref[0, :, cols], k2_ref[0, :, cols]], axis=0)
            v = jnp.concatenate([v0_ref[0, :, cols], v1_ref[0, :, cols], v2_ref[0, :, cols]], axis=0)
            s = _nt_dot(q, k) + bias_ref[h]
            if mask_left:
                kpos = (i - 2) * A_TQ + lax.broadcasted_iota(jnp.int32, s.shape, 1)
                s = jnp.where(kpos >= 0, s, NEG)
            m = jnp.max(s, axis=-1, keepdims=True)
            p = jnp.exp2(s - m)
            l = jnp.sum(p, axis=-1, keepdims=True)
            o = jnp.dot(p.astype(BF16), v, preferred_element_type=F32) / l
            o_ref[0, :, cols] = o.astype(BF16)

    @pl.when(i < 2)
    def _():
        tile(True)

    @pl.when(i >= 2)
    def _():
        tile(False)


def _attn_a(pj1, diag, layer):
    b, s, _ = pj1.shape
    blk = (1, A_TQ, D_A)

    def kv_spec(col, back):
        return pl.BlockSpec(blk, lambda bb, i: (bb, jnp.maximum(i - back, 0), col))

    return pl.pallas_call(
        _attn_a_kernel,
        grid=(b, s // A_TQ),
        in_specs=[
            pl.BlockSpec(blk, lambda bb, i: (bb, i, _QA_BLK)),
            kv_spec(_KA_BLK, 2), kv_spec(_KA_BLK, 1), kv_spec(_KA_BLK, 0),
            kv_spec(_VA_BLK, 2), kv_spec(_VA_BLK, 1), kv_spec(_VA_BLK, 0),
            pl.BlockSpec((None, H_A, A_DIAG), lambda bb, i: (layer, 0, 0)),
        ],
        out_specs=pl.BlockSpec(blk, lambda bb, i: (bb, i, 0)),
        out_shape=jax.ShapeDtypeStruct((b, s, D_A), BF16),
        scratch_shapes=[pltpu.VMEM((H_A, A_TQ, 3 * A_TQ), F32)],
        compiler_params=pltpu.CompilerParams(
            dimension_semantics=("arbitrary", "arbitrary"), vmem_limit_bytes=_VMEM_LIMIT),
        name="attn_a",
    )(pj1, pj1, pj1, pj1, pj1, pj1, pj1, diag)


def _float_key(v):
    bits = int(np.float32(v).view(np.int32))
    return bits if bits >= 0 else bits ^ 0x7FFFFFFF


_KEY_LO = _float_key(NEG)
_KEY_HI = 0x7F800001


def _key_to_float(k):
    bits = jnp.where(k >= 0, k, k ^ 0x7FFFFFFF)
    return lax.bitcast_convert_type(bits, F32)


def _float_to_key(f):
    bits = lax.bitcast_convert_type(f, jnp.int32)
    return jnp.where(bits >= 0, bits, bits ^ 0x7FFFFFFF)


def _topk_mask_bias(score_ref, bias_ref, nk, topk, n_adm, side_work):
    tq = score_ref.shape[0]
    nt = nk // LANES
    shape = (tq, LANES)
    row_groups = [slice(g * LANES, (g + 1) * LANES) for g in range(tq // LANES)]

    def tile(rows, t):
        return score_ref[rows, t * LANES:(t + 1) * LANES]

    def row_reduce(tile_fn, combine, lane_reduce):
        out = []
        for rows in row_groups:
            acc = tile_fn(tile(rows, 0), 0, rows)
            for t in range(1, nt):
                acc = combine(acc, tile_fn(tile(rows, t), t, rows))
            out.append(jnp.broadcast_to(lane_reduce(acc, axis=1, keepdims=True), (LANES, LANES)))
        return jnp.concatenate(out, axis=0)

    def count(pred):
        return row_reduce(lambda sc, t, rows: jnp.where(pred(sc, t, rows), 1.0, 0.0),
                          jnp.add, jnp.sum)

    def step(lo, hi, c_lo, mid):
        th = _key_to_float(mid)
        c = count(lambda sc, t, rows: sc >= th[rows])
        ge = c >= topk
        return jnp.where(ge, mid, lo), jnp.where(ge, hi, mid), jnp.where(ge, c, c_lo)

    row_max = row_reduce(lambda sc, t, rows: sc, jnp.maximum, jnp.max)
    row_min = row_reduce(lambda sc, t, rows: jnp.where(sc > 0.5 * NEG, sc, -NEG), jnp.minimum, jnp.min)
    lo0 = _float_to_key(row_min)
    hi0 = jnp.minimum(_float_to_key(row_max), _KEY_HI - 2) + 1
    few = n_adm < topk

    def value_round(r, carry):
        lo, hi, c_lo = carry
        for _ in range(_SEL_PASSES):
            mid_f = 0.5 * _key_to_float(lo) + 0.5 * _key_to_float(hi)
            mid = jnp.minimum(jnp.maximum(_float_to_key(mid_f), lo + 1), hi - 1)
            lo, hi, c_lo = step(lo, hi, c_lo, mid)
        side_work(r)
        return lo, hi, c_lo

    lo, hi, c_lo = lax.fori_loop(0, _SEL_ROUNDS, value_round, (lo0, hi0, n_adm))

    def n_open(lo, hi, c_lo):
        closed = few | (c_lo == topk) | (lax.shift_right_logical(hi - lo, 1) == 0)
        o = jnp.where(closed, 0.0, 1.0)
        return jnp.max(jnp.max(o, axis=1, keepdims=True), axis=0, keepdims=True)[0, 0]

    def bit_pass(carry):
        it, lo, hi, c_lo, _ = carry
        lo, hi, c_lo = step(lo, hi, c_lo, lo + lax.shift_right_logical(hi - lo, 1))
        return it + 1, lo, hi, c_lo, n_open(lo, hi, c_lo)

    _, lo, _, c_lo, _ = lax.while_loop(
        lambda carry: (carry[4] > 0.0) & (carry[0] < 40), bit_pass,
        (jnp.int32(0), lo, hi, c_lo, n_open(lo, hi, c_lo)))
    thr = _key_to_float(lo)
    partial = jnp.where((c_lo > topk) & (thr > 0.5 * NEG), 1.0, 0.0)
    any_partial = jnp.max(jnp.max(partial, axis=1, keepdims=True), axis=0, keepdims=True)[0, 0] > 0.0

    @pl.when(jnp.logical_not(any_partial))
    def _():
        for rows in row_groups:
            for t in range(nt):
                sc = tile(rows, t)
                sel = (sc >= thr[rows]) & (sc > 0.5 * NEG)
                bias_ref[rows, t * LANES:(t + 1) * LANES] = jnp.where(sel, 0.0, NEG)

    @pl.when(any_partial)
    def _():
        need = topk - count(lambda sc, t, rows: sc > thr[rows])
        lane = lax.broadcasted_iota(jnp.int32, (LANES, LANES), 1)

        def tie_bisect(b, jsel):
            cand = jsel + jnp.left_shift(jnp.int32(1), 10 - b)
            c = count(lambda sc, t, rows: (sc == thr[rows]) & (lane + t * LANES < cand[rows]))
            return jnp.where(c < need, cand, jsel)

        jsel = lax.fori_loop(0, 11, tie_bisect, jnp.zeros(shape, jnp.int32))
        for rows in row_groups:
            for t in range(nt):
                sc = tile(rows, t)
                sel = (sc > thr[rows]) | ((sc == thr[rows]) & (lane + t * LANES <= jsel[rows]))
                sel = sel & (sc > 0.5 * NEG)
                bias_ref[rows, t * LANES:(t + 1) * LANES] = jnp.where(sel, 0.0, NEG)


def _attn_b_block(nk, i, qb_ref, kb_ref, vb_ref, qiw_ref, kiw_ref, o_ref, score_ref, bias_ref,
                  s_ref, qh_ref, topk):
    tq = score_ref.shape[0]
    ki = kiw_ref[0, :nk, :D_IDX]
    vb = vb_ref[0, :nk, :]
    wi = qiw_ref[0, :, _WI_OFF:_WI_OFF + H_IDX].astype(F32)

    score = jnp.zeros((tq, nk), F32)
    for h in range(H_IDX):
        qh = qiw_ref[0, :, h * D_IDX:(h + 1) * D_IDX]
        score = score + wi[:, h:h + 1] * jnp.maximum(_nt_dot(qh, ki), 0.0)
    t_pos = i * tq + lax.broadcasted_iota(jnp.int32, (tq, nk), 0)
    visible_end = (t_pos // CHUNK + 1) * CHUNK
    key_pos = lax.broadcasted_iota(jnp.int32, (tq, nk), 1)
    score_ref[:, :nk] = jnp.where(key_pos < visible_end, score, NEG)
    for h in range(H_B):
        qh_ref[h] = qb_ref[0, :, h * HEAD_DIM:(h + 1) * HEAD_DIM]

    def qk_logits(r):
        for j in range(H_B // _SEL_ROUNDS):
            h = r * (H_B // _SEL_ROUNDS) + j
            s_ref[h, :, :nk] = _nt_dot(qh_ref[h], kb_ref[0, :nk, :])

    row_pos = i * tq + lax.broadcasted_iota(jnp.int32, (tq, LANES), 0)
    n_adm = ((row_pos // CHUNK + 1) * CHUNK).astype(F32)
    _topk_mask_bias(score_ref, bias_ref, nk, topk, n_adm, qk_logits)

    for h in range(H_B):
        sc = s_ref[h, :, :nk] + bias_ref[:, :nk]
        m = jnp.max(sc, axis=-1, keepdims=True)
        p = jnp.exp2(sc - m)
        l = jnp.sum(p, axis=-1, keepdims=True)
        o = jnp.dot(p.astype(BF16), vb, preferred_element_type=F32) / l
        o_ref[0, :, h * HEAD_DIM:(h + 1) * HEAD_DIM] = o.astype(BF16)


def _attn_b_kernel(qb_ref, kb_ref, vb_ref, qiw_ref, kiw_ref, o_ref, score_ref, bias_ref, s_ref,
                   qh_ref, *, topk):
    i = pl.program_id(1)
    tq, s = score_ref.shape
    per = B_KSTEP // tq
    for n in range(1, s // B_KSTEP + 1):
        @pl.when(i // per == n - 1)
        def _(n=n):
            _attn_b_block(n * B_KSTEP, i, qb_ref, kb_ref, vb_ref, qiw_ref, kiw_ref, o_ref,
                          score_ref, bias_ref, s_ref, qh_ref, topk)


def _attn_b(pj2, topk):
    b, s, _ = pj2.shape
    return pl.pallas_call(
        functools.partial(_attn_b_kernel, topk=topk),
        grid=(b, s // B_TQ),
        in_specs=[
            pl.BlockSpec((1, B_TQ, D_B), lambda bb, i: (bb, i, _QB_BLK)),
            pl.BlockSpec((1, s, HEAD_DIM), lambda bb, i: (bb, 0, _KB_BLK)),
            pl.BlockSpec((1, s, HEAD_DIM), lambda bb, i: (bb, 0, _VB_BLK)),
            pl.BlockSpec((1, B_TQ, P_SUB), lambda bb, i: (bb, i, _QI_BLK)),
            pl.BlockSpec((1, s, LANES), lambda bb, i: (bb, 0, _KI_BLK)),
        ],
        out_specs=pl.BlockSpec((1, B_TQ, D_B), lambda bb, i: (bb, i, 0)),
        out_shape=jax.ShapeDtypeStruct((b, s, D_B), BF16),
        scratch_shapes=[pltpu.VMEM((B_TQ, s), F32), pltpu.VMEM((B_TQ, s), F32),
                        pltpu.VMEM((H_B, B_TQ, s), F32), pltpu.VMEM((H_B, B_TQ, HEAD_DIM), BF16)],
        compiler_params=pltpu.CompilerParams(
            dimension_semantics=("parallel", "parallel"), vmem_limit_bytes=_VMEM_LIMIT),
        name="attn_b",
    )(pj2, pj2, pj2, pj2, pj2)


def _sigmoid(v):
    return 1.0 / (1.0 + jnp.exp(-v))


def _out_ln_kernel(x_ref, p_ref, ya_ref, yb_ref, ga_ref, gb_ref, wo_ref, wg_ref, bg_ref, wp_ref,
                   lg_ref, lb_ref, o_ref, ob_ref):
    x = x_ref[...]
    ga = ga_ref[...].astype(F32)
    gb = gb_ref[...].astype(F32)
    ua = (ya_ref[...].astype(F32) * (ga * _sigmoid(ga))).astype(BF16)
    ub = (yb_ref[...].astype(F32) * (gb * _sigmoid(gb))).astype(BF16)
    y = (jnp.dot(ua, wo_ref[:D_A, :], preferred_element_type=F32)
         + jnp.dot(ub, wo_ref[D_A:, :], preferred_element_type=F32))
    gate = _sigmoid(jnp.dot(x.astype(BF16), wg_ref[...], preferred_element_type=F32) + bg_ref[...])
    ple = gate * jnp.dot(p_ref[...].astype(BF16), wp_ref[...], preferred_element_type=F32)
    z = ALPHA * x + y + ple
    mu = jnp.mean(z, axis=-1, keepdims=True)
    zc = z - mu
    var = jnp.mean(zc * zc, axis=-1, keepdims=True)
    out = zc * lax.rsqrt(var + LN_EPS) * lg_ref[...] + lb_ref[...]
    o_ref[...] = out
    ob_ref[...] = out.astype(BF16)


def _out_ln(x2d, p3, ya2d, yb2d, pj1, wo, wg, bg3, wp, lg3, lb3, layer):
    m = x2d.shape[0]
    row = lambda i: (i, 0)
    fixed = lambda i: (layer, 0, 0)
    once = pl.Buffered(1)
    return pl.pallas_call(
        _out_ln_kernel,
        grid=(m // OUT_TM,),
        in_specs=[
            pl.BlockSpec((OUT_TM, D_MODEL), row),
            pl.BlockSpec((None, OUT_TM, D_PLE), lambda i: (layer, i, 0)),
            pl.BlockSpec((OUT_TM, D_A), row),
            pl.BlockSpec((OUT_TM, D_B), row),
            pl.BlockSpec((OUT_TM, D_A), lambda i: (i, _GA_BLK)),
            pl.BlockSpec((OUT_TM, D_B), lambda i: (i, _GB_BLK)),
            pl.BlockSpec((None, D_MODEL, D_MODEL), fixed, pipeline_mode=once),
            pl.BlockSpec((None, D_MODEL, D_MODEL), fixed, pipeline_mode=once),
            pl.BlockSpec((None, 1, D_MODEL), fixed),
            pl.BlockSpec((None, D_PLE, D_MODEL), fixed, pipeline_mode=once),
            pl.BlockSpec((None, 1, D_MODEL), fixed),
            pl.BlockSpec((None, 1, D_MODEL), fixed),
        ],
        out_specs=[pl.BlockSpec((OUT_TM, D_MODEL), row), pl.BlockSpec((OUT_TM, D_MODEL), row)],
        out_shape=[jax.ShapeDtypeStruct((m, D_MODEL), F32), jax.ShapeDtypeStruct((m, D_MODEL), BF16)],
        compiler_params=pltpu.CompilerParams(
            dimension_semantics=("parallel",), vmem_limit_bytes=_VMEM_LIMIT),
        name="out_ln",
    )(x2d, p3, ya2d, yb2d, pj1, pj1, wo, wg, bg3, wp, lg3, lb3)


def _rope_col_scale():
    sc = np.ones((1, D_P2), np.float32)
    sc[:, :D_B] = Q_SCALE
    qi0 = D_B + 2 * HEAD_DIM
    sc[:, qi0:qi0 + H_IDX * D_IDX] = QI_SCALE
    wi0 = qi0 + H_IDX * D_IDX + D_IDX
    sc[:, wi0:wi0 + H_IDX] = WI_SCALE
    return jnp.asarray(sc)


def _rope_tables(positions):
    def cos_sin(dim):
        inv = ROPE_THETA ** (-jnp.arange(0, dim, 2, dtype=F32) / dim)
        ang = positions.astype(F32)[..., None] * inv
        ang = jnp.concatenate([ang, ang], -1).reshape(-1, dim)
        return jnp.cos(ang), jnp.sin(ang)

    cos_h, sin_h = cos_sin(HEAD_DIM)
    lane = jnp.arange(LANES)
    sin_h = jnp.where(lane < HEAD_DIM // 2, -sin_h, sin_h)
    cos_i, sin_i = cos_sin(D_IDX)
    cos_i = jnp.concatenate([cos_i, cos_i], -1)
    sin_i = jnp.concatenate([sin_i, sin_i], -1)
    first_half = (lane % D_IDX) < D_IDX // 2
    sa_i = jnp.where(first_half, -sin_i, 0.0)
    sb_i = jnp.where(first_half, 0.0, sin_i)
    is_ki = lane < D_IDX
    cos_l = jnp.where(is_ki, cos_i, 1.0)
    sa_l = jnp.where(is_ki, sa_i, 0.0)
    sb_l = jnp.where(is_ki, sb_i, 0.0)
    return jnp.concatenate([cos_h, sin_h, cos_i, sa_i, sb_i, cos_l, sa_l, sb_l], axis=-1)


def _band_bias_diagonals(rel_bias):
    lead = rel_bias.shape[:-1]
    n_far = 3 * A_TQ - 1 - REL_CLIP
    n_near = A_DIAG - n_far - (2 * REL_CLIP + 1)
    return jnp.concatenate([
        jnp.broadcast_to(rel_bias[..., -1:], lead + (n_far,)),
        rel_bias[..., ::-1],
        jnp.broadcast_to(rel_bias[..., :1], lead + (n_near,))], axis=-1) * LOG2E


def kernel(x, p, positions, w_in, b_in, rel_bias, w_out, w_ple, w_ple_gate, b_ple_gate, ln_g, ln_b):
    b, s, d = x.shape
    m = b * s
    assert d == D_MODEL and s % B_KSTEP == 0 and m % P1_TM == 0
    topk = min(TOPK_MAX, s // 4)
    tabs = _rope_tables(positions)
    col_scale = _rope_col_scale()
    diag = _band_bias_diagonals(rel_bias)
    w_in_t = jnp.swapaxes(w_in, 1, 2)
    b_in3 = b_in[:, None, :]
    wo, wg, wp = w_out.astype(BF16), w_ple_gate.astype(BF16), w_ple.astype(BF16)
    bg3, lg3, lb3 = b_ple_gate[:, None, :], ln_g[:, None, :], ln_b[:, None, :]
    p3 = p.reshape(DEPTH, m, D_PLE)
    x2d = x.reshape(m, d)
    xb = x2d.astype(BF16)
    for i in range(DEPTH):
        pj1 = _proj_plain(xb, w_in_t, b_in3, i)
        pj2 = _proj_rope(xb, w_in_t, b_in3, col_scale, tabs, i)
        ya = _attn_a(pj1.reshape(b, s, D_P1), diag, i)
        yb = _attn_b(pj2.reshape(b, s, D_P2), topk)
        x2d, xb = _out_ln(x2d, p3, ya.reshape(m, D_A), yb.reshape(m, D_B), pj1,
                          wo, wg, bg3, wp, lg3, lb3, i)
    return x2d.reshape(b, s, d)
```

```python
import functools

import numpy as np
import jax
import jax.numpy as jnp
from jax import lax
from jax.experimental import pallas as pl
from jax.experimental.pallas import tpu as pltpu

D_MODEL = 2048
DEPTH = 4
CHUNK = 64
LEFT_CHUNKS = 8
HEAD_DIM = 128
D_A = 1024
D_B = 1024
H_A = 8
H_B = 8
REL_CLIP = 128
H_IDX = 8
D_IDX = 64
TOPK_MAX = 256
D_PLE = 256
ROPE_THETA = 10000.0
LN_EPS = 1e-5
NEG = -1e30
ALPHA = (2.0 * DEPTH) ** 0.25
LOG2E = 1.4426950408889634

F32 = jnp.float32
BF16 = jnp.bfloat16
LANES = 128

D_IN = 6984
W_BLK = 256
Q_SCALE = HEAD_DIM ** -0.5 * LOG2E
QI_SCALE = D_IDX ** -0.5
WI_SCALE = H_IDX ** -0.5

D_P1 = 5120
_QA_BLK, _KA_BLK, _VA_BLK, _GA_BLK, _GB_BLK = 0, 1, 2, 3, 4
_P1_SPLIT = 4096 // W_BLK
_P1_SKIP = (5376 - 4096) // W_BLK
_P2_SRC = (16, 17, 18, 19, 20, 25, 26, 27)
D_P2 = 7 * W_BLK + LANES
_P2_TAIL_VALID = D_IN - 27 * W_BLK
_QB_BLK = 0
_KB_BLK, _VB_BLK, _KI_BLK = 8, 9, 14
_QI_BLK = 2
_WI_OFF = H_IDX * D_IDX + D_IDX
P_SUB = 640
_P2_KINDS = ('hhhhh', 'hhhhp', 'iiiil')

P1_TM, P1_TN = 2048, 2 * W_BLK
P2_TM = 512
A_TQ = 256
A_DIAG = 4 * A_TQ
B_TQ = 256
B_KSTEP = 512
_SEL_ROUNDS, _SEL_PASSES = 8, 4
OUT_TM = 256

_VMEM_LIMIT = 56 * 1024 * 1024


def _nt_dot(a, b):
    return lax.dot_general(a, b, (((1,), (1,)), ((), ())), preferred_element_type=F32)


def _proj_plain_kernel(x_ref, wa_ref, wb_ref, ba_ref, bb_ref, o_ref):
    j = pl.program_id(1)
    w = jnp.concatenate([wa_ref[...].astype(BF16), wb_ref[...].astype(BF16)], axis=0)
    bias = jnp.concatenate([ba_ref[...], bb_ref[...]], axis=1)
    scale = jnp.where(j < D_A // P1_TN, Q_SCALE, 1.0)
    half = P1_TM // 2
    for r in range(2):
        rows = slice(r * half, (r + 1) * half)
        acc = _nt_dot(x_ref[rows, :], w) + bias
        o_ref[rows, :] = (acc * scale).astype(BF16)


def _proj_plain(xb, w_in_t, b_in3, layer):
    m = xb.shape[0]

    def src(u):
        return u + jnp.where(u >= _P1_SPLIT, _P1_SKIP, 0)

    def wspec(k):
        return pl.BlockSpec((None, W_BLK, D_MODEL), lambda i, j: (layer, src(2 * j + k), 0))

    def bspec(k):
        return pl.BlockSpec((None, 1, W_BLK), lambda i, j: (layer, 0, src(2 * j + k)))

    return pl.pallas_call(
        _proj_plain_kernel,
        grid=(m // P1_TM, D_P1 // P1_TN),
        in_specs=[pl.BlockSpec((P1_TM, D_MODEL), lambda i, j: (i, 0)),
                  wspec(0), wspec(1), bspec(0), bspec(1)],
        out_specs=pl.BlockSpec((P1_TM, P1_TN), lambda i, j: (i, j)),
        out_shape=jax.ShapeDtypeStruct((m, D_P1), BF16),
        compiler_params=pltpu.CompilerParams(
            dimension_semantics=("parallel", "arbitrary"), vmem_limit_bytes=_VMEM_LIMIT),
        name="proj_plain",
    )(xb, w_in_t, w_in_t, b_in3, b_in3)


def _proj_rope_kernel(x_ref, *refs):
    nsrc = len(_P2_SRC)
    w_refs, b_refs = refs[:nsrc], refs[nsrc:2 * nsrc]
    scale_ref, tab_ref, o_ref, wbf_ref, bias_ref = refs[2 * nsrc:]

    @pl.when(pl.program_id(0) == 0)
    def _():
        for n in range(nsrc):
            width = W_BLK if n < nsrc - 1 else LANES
            cols = slice(n * W_BLK, n * W_BLK + width)
            w = w_refs[n][:width, :]
            b = b_refs[n][:, :width]
            if n == nsrc - 1:
                w = jnp.where(lax.broadcasted_iota(jnp.int32, (width, 1), 0) < _P2_TAIL_VALID, w, 0.0)
                b = jnp.where(lax.broadcasted_iota(jnp.int32, (1, width), 1) < _P2_TAIL_VALID, b, 0.0)
            wbf_ref[cols, :] = w.astype(BF16)
            bias_ref[:, cols] = b * scale_ref[:, cols]

    x = x_ref[...]

    def tab(k):
        return tab_ref[:, k * LANES:(k + 1) * LANES]

    def rope_head(t):
        return t * tab(0) + pltpu.roll(t, 64, 1) * tab(1)

    def rope_idx(t, base):
        return (t * tab(base) + pltpu.roll(t, 96, 1) * tab(base + 1)
                + pltpu.roll(t, 32, 1) * tab(base + 2))

    for s, kinds in enumerate(_P2_KINDS):
        cols = slice(s * P_SUB, (s + 1) * P_SUB)
        acc = _nt_dot(x, wbf_ref[cols, :]) * scale_ref[:, cols] + bias_ref[:, cols]
        for k, kind in enumerate(kinds):
            t = acc[:, k * LANES:(k + 1) * LANES]
            if kind == 'h':
                t = rope_head(t)
            elif kind == 'i':
                t = rope_idx(t, 2)
            elif kind == 'l':
                t = rope_idx(t, 5)
            c0 = s * P_SUB + k * LANES
            o_ref[:, c0:c0 + LANES] = t.astype(BF16)


def _proj_rope(xb, w_in_t, b_in3, col_scale, tabs, layer):
    m = xb.shape[0]
    once = pl.Buffered(1)
    wspecs = [pl.BlockSpec((None, W_BLK, D_MODEL), lambda i, u=u: (layer, u, 0), pipeline_mode=once)
              for u in _P2_SRC]
    bspecs = [pl.BlockSpec((None, 1, W_BLK), lambda i, u=u: (layer, 0, u)) for u in _P2_SRC]
    nsrc = len(_P2_SRC)
    return pl.pallas_call(
        _proj_rope_kernel,
        grid=(m // P2_TM,),
        in_specs=[pl.BlockSpec((P2_TM, D_MODEL), lambda i: (i, 0))] + wspecs + bspecs + [
            pl.BlockSpec((1, D_P2), lambda i: (0, 0)),
            pl.BlockSpec((P2_TM, 8 * LANES), lambda i: (i, 0)),
        ],
        out_specs=pl.BlockSpec((P2_TM, D_P2), lambda i: (i, 0)),
        out_shape=jax.ShapeDtypeStruct((m, D_P2), BF16),
        scratch_shapes=[pltpu.VMEM((D_P2, D_MODEL), BF16), pltpu.VMEM((1, D_P2), F32)],
        compiler_params=pltpu.CompilerParams(
            dimension_semantics=("arbitrary",), vmem_limit_bytes=_VMEM_LIMIT),
        name="proj_rope",
    )(xb, *([w_in_t] * nsrc), *([b_in3] * nsrc), col_scale, tabs)


def _attn_a_kernel(q_ref, k0_ref, k1_ref, k2_ref, v0_ref, v1_ref, v2_ref, diag_ref, o_ref, bias_ref):
    i = pl.program_id(1)
    nk = 3 * A_TQ

    @pl.when((pl.program_id(0) == 0) & (i == 0))
    def _():
        qc = lax.shift_right_logical(lax.broadcasted_iota(jnp.int32, (A_TQ, nk), 0), 6)
        kc = lax.shift_right_logical(lax.broadcasted_iota(jnp.int32, (A_TQ, nk), 1), 6)
        band = (kc >= qc) & (kc <= qc + LEFT_CHUNKS)
        for h in range(H_A):
            rows = jnp.broadcast_to(diag_ref[h:h + 1, :], (A_TQ, A_DIAG))
            t = pltpu.roll(rows, A_DIAG - (A_TQ - 1), 1, stride=1, stride_axis=0)
            bias_ref[h] = jnp.where(band, t[:, :nk], NEG)

    def tile(mask_left):
        for h in range(H_A):
            cols = slice(h * HEAD_DIM, (h + 1) * HEAD_DIM)
            q = q_ref[0, :, cols]
            k = jnp.concatenate([k0_ref[0, :, cols], k1_ref[0, :, cols], k2_ref[0, :, cols]], axis=0)
            v = jnp.concatenate([v0_ref[0, :, cols], v1_ref[0, :, cols], v2_ref[0, :, cols]], axis=0)
            s = _nt_dot(q, k) + bias_ref[h]
            if mask_left:
                kpos = (i - 2) * A_TQ + lax.broadcasted_iota(jnp.int32, s.shape, 1)
                s = jnp.where(kpos >= 0, s, NEG)
            m = jnp.max(s, axis=-1, keepdims=True)
            p = jnp.exp2(s - m)
            l = jnp.sum(p, axis=-1, keepdims=True)
            o = jnp.dot(p.astype(BF16), v, preferred_element_type=F32) / l
            o_ref[0, :, cols] = o.astype(BF16)

    @pl.when(i < 2)
    def _():
        tile(True)

    @pl.when(i >= 2)
    def _():
        tile(False)


def _attn_a(pj1, diag, layer):
    b, s, _ = pj1.shape
    blk = (1, A_TQ, D_A)

    def kv_spec(col, back):
        return pl.BlockSpec(blk, lambda bb, i: (bb, jnp.maximum(i - back, 0), col))

    return pl.pallas_call(
        _attn_a_kernel,
        grid=(b, s // A_TQ),
        in_specs=[
            pl.BlockSpec(blk, lambda bb, i: (bb, i, _QA_BLK)),
            kv_spec(_KA_BLK, 2), kv_spec(_KA_BLK, 1), kv_spec(_KA_BLK, 0),
            kv_spec(_VA_BLK, 2), kv_spec(_VA_BLK, 1), kv_spec(_VA_BLK, 0),
            pl.BlockSpec((None, H_A, A_DIAG), lambda bb, i: (layer, 0, 0)),
        ],
        out_specs=pl.BlockSpec(blk, lambda bb, i: (bb, i, 0)),
        out_shape=jax.ShapeDtypeStruct((b, s, D_A), BF16),
        scratch_shapes=[pltpu.VMEM((H_A, A_TQ, 3 * A_TQ), F32)],
        compiler_params=pltpu.CompilerParams(
            dimension_semantics=("arbitrary", "arbitrary"), vmem_limit_bytes=_VMEM_LIMIT),
        name="attn_a",
    )(pj1, pj1, pj1, pj1, pj1, pj1, pj1, diag)


def _float_key(v):
    bits = int(np.float32(v).view(np.int32))
    return bits if bits >= 0 else bits ^ 0x7FFFFFFF


_KEY_LO = _float_key(NEG)
_KEY_HI = 0x7F800001


def _key_to_float(k):
    bits = jnp.where(k >= 0, k, k ^ 0x7FFFFFFF)
    return lax.bitcast_convert_type(bits, F32)


def _topk_mask_bias(score_ref, bias_ref, nk, topk, side_work):
    tq = score_ref.shape[0]
    nt = nk // LANES
    shape = (tq, LANES)
    row_groups = [slice(g * LANES, (g + 1) * LANES) for g in range(tq // LANES)]

    def tile(rows, t):
        return score_ref[rows, t * LANES:(t + 1) * LANES]

    def row_reduce(tile_fn, combine, lane_reduce):
        out = []
        for rows in row_groups:
            acc = tile_fn(tile(rows, 0), 0, rows)
            for t in range(1, nt):
                acc = combine(acc, tile_fn(tile(rows, t), t, rows))
            out.append(jnp.broadcast_to(lane_reduce(acc, axis=1, keepdims=True), (LANES, LANES)))
        return jnp.concatenate(out, axis=0)

    def count(pred):
        return row_reduce(lambda sc, t, rows: jnp.where(pred(sc, t, rows), 1.0, 0.0),
                          jnp.add, jnp.sum)

    def step(lo, hi, c_lo, mid):
        th = _key_to_float(mid)
        c = count(lambda sc, t, rows: sc >= th[rows])
        ge = c >= topk
        return jnp.where(ge, mid, lo), jnp.where(ge, hi, mid), jnp.where(ge, c, c_lo)

    def bisect_round(r, carry):
        lo, hi, c_lo = carry
        for _ in range(_SEL_PASSES):
            lo, hi, c_lo = step(lo, hi, c_lo, lo + lax.shift_right_logical(hi - lo, 1))
        side_work(r)
        return lo, hi, c_lo

    lo, _, c_lo = lax.fori_loop(
        0, _SEL_ROUNDS, bisect_round,
        (jnp.full(shape, _KEY_LO, jnp.int32), jnp.full(shape, _KEY_HI, jnp.int32),
         jnp.full(shape, float(nk), F32)))
    thr = _key_to_float(lo)
    partial = jnp.where((c_lo > topk) & (thr > 0.5 * NEG), 1.0, 0.0)
    any_partial = jnp.max(jnp.max(partial, axis=1, keepdims=True), axis=0, keepdims=True)[0, 0] > 0.0

    @pl.when(jnp.logical_not(any_partial))
    def _():
        for rows in row_groups:
            for t in range(nt):
                sc = tile(rows, t)
                sel = (sc >= thr[rows]) & (sc > 0.5 * NEG)
                bias_ref[rows, t * LANES:(t + 1) * LANES] = jnp.where(sel, 0.0, NEG)

    @pl.when(any_partial)
    def _():
        need = topk - count(lambda sc, t, rows: sc > thr[rows])
        lane = lax.broadcasted_iota(jnp.int32, (LANES, LANES), 1)

        def tie_bisect(b, jsel):
            cand = jsel + jnp.left_shift(jnp.int32(1), 10 - b)
            c = count(lambda sc, t, rows: (sc == thr[rows]) & (lane + t * LANES < cand[rows]))
            return jnp.where(c < need, cand, jsel)

        jsel = lax.fori_loop(0, 11, tie_bisect, jnp.zeros(shape, jnp.int32))
        for rows in row_groups:
            for t in range(nt):
                sc = tile(rows, t)
                sel = (sc > thr[rows]) | ((sc == thr[rows]) & (lane + t * LANES <= jsel[rows]))
                sel = sel & (sc > 0.5 * NEG)
                bias_ref[rows, t * LANES:(t + 1) * LANES] = jnp.where(sel, 0.0, NEG)


def _attn_b_block(nk, i, qb_ref, kb_ref, vb_ref, qiw_ref, kiw_ref, o_ref, score_ref, bias_ref,
                  s_ref, qh_ref, topk):
    tq = score_ref.shape[0]
    ki = kiw_ref[0, :nk, :D_IDX]
    vb = vb_ref[0, :nk, :]
    wi = qiw_ref[0, :, _WI_OFF:_WI_OFF + H_IDX].astype(F32)

    score = jnp.zeros((tq, nk), F32)
    for h in range(H_IDX):
        qh = qiw_ref[0, :, h * D_IDX:(h + 1) * D_IDX]
        score = score + wi[:, h:h + 1] * jnp.maximum(_nt_dot(qh, ki), 0.0)
    t_pos = i * tq + lax.broadcasted_iota(jnp.int32, (tq, nk), 0)
    visible_end = (t_pos // CHUNK + 1) * CHUNK
    key_pos = lax.broadcasted_iota(jnp.int32, (tq, nk), 1)
    score_ref[:, :nk] = jnp.where(key_pos < visible_end, score, NEG)
    for h in range(H_B):
        qh_ref[h] = qb_ref[0, :, h * HEAD_DIM:(h + 1) * HEAD_DIM]

    def qk_logits(r):
        for j in range(H_B // _SEL_ROUNDS):
            h = r * (H_B // _SEL_ROUNDS) + j
            s_ref[h, :, :nk] = _nt_dot(qh_ref[h], kb_ref[0, :nk, :])

    _topk_mask_bias(score_ref, bias_ref, nk, topk, qk_logits)

    for h in range(H_B):
        sc = s_ref[h, :, :nk] + bias_ref[:, :nk]
        m = jnp.max(sc, axis=-1, keepdims=True)
        p = jnp.exp2(sc - m)
        l = jnp.sum(p, axis=-1, keepdims=True)
        o = jnp.dot(p.astype(BF16), vb, preferred_element_type=F32) / l
        o_ref[0, :, h * HEAD_DIM:(h + 1) * HEAD_DIM] = o.astype(BF16)


def _attn_b_kernel(qb_ref, kb_ref, vb_ref, qiw_ref, kiw_ref, o_ref, score_ref, bias_ref, s_ref,
                   qh_ref, *, topk):
    i = pl.program_id(1)
    tq, s = score_ref.shape
    per = B_KSTEP // tq
    for n in range(1, s // B_KSTEP + 1):
        @pl.when(i // per == n - 1)
        def _(n=n):
            _attn_b_block(n * B_KSTEP, i, qb_ref, kb_ref, vb_ref, qiw_ref, kiw_ref, o_ref,
                          score_ref, bias_ref, s_ref, qh_ref, topk)


def _attn_b(pj2, topk):
    b, s, _ = pj2.shape
    return pl.pallas_call(
        functools.partial(_attn_b_kernel, topk=topk),
        grid=(b, s // B_TQ),
        in_specs=[
            pl.BlockSpec((1, B_TQ, D_B), lambda bb, i: (bb, i, _QB_BLK)),
            pl.BlockSpec((1, s, HEAD_DIM), lambda bb, i: (bb, 0, _KB_BLK)),
            pl.BlockSpec((1, s, HEAD_DIM), lambda bb, i: (bb, 0, _VB_BLK)),
            pl.BlockSpec((1, B_TQ, P_SUB), lambda bb, i: (bb, i, _QI_BLK)),
            pl.BlockSpec((1, s, LANES), lambda bb, i: (bb, 0, _KI_BLK)),
        ],
        out_specs=pl.BlockSpec((1, B_TQ, D_B), lambda bb, i: (bb, i, 0)),
        out_shape=jax.ShapeDtypeStruct((b, s, D_B), BF16),
        scratch_shapes=[pltpu.VMEM((B_TQ, s), F32), pltpu.VMEM((B_TQ, s), F32),
                        pltpu.VMEM((H_B, B_TQ, s), F32), pltpu.VMEM((H_B, B_TQ, HEAD_DIM), BF16)],
        compiler_params=pltpu.CompilerParams(
            dimension_semantics=("parallel", "parallel"), vmem_limit_bytes=_VMEM_LIMIT),
        name="attn_b",
    )(pj2, pj2, pj2, pj2, pj2)


def _sigmoid(v):
    return 1.0 / (1.0 + jnp.exp(-v))


def _out_ln_kernel(x_ref, p_ref, ya_ref, yb_ref, ga_ref, gb_ref, wo_ref, wg_ref, bg_ref, wp_ref,
                   lg_ref, lb_ref, o_ref, ob_ref):
    x = x_ref[...]
    ga = ga_ref[...].astype(F32)
    gb = gb_ref[...].astype(F32)
    ua = (ya_ref[...].astype(F32) * (ga * _sigmoid(ga))).astype(BF16)
    ub = (yb_ref[...].astype(F32) * (gb * _sigmoid(gb))).astype(BF16)
    y = (jnp.dot(ua, wo_ref[:D_A, :], preferred_element_type=F32)
         + jnp.dot(ub, wo_ref[D_A:, :], preferred_element_type=F32))
    gate = _sigmoid(jnp.dot(x.astype(BF16), wg_ref[...], preferred_element_type=F32) + bg_ref[...])
    ple = gate * jnp.dot(p_ref[...].astype(BF16), wp_ref[...], preferred_element_type=F32)
    z = ALPHA * x + y + ple
    mu = jnp.mean(z, axis=-1, keepdims=True)
    zc = z - mu
    var = jnp.mean(zc * zc, axis=-1, keepdims=True)
    out = zc * lax.rsqrt(var + LN_EPS) * lg_ref[...] + lb_ref[...]
    o_ref[...] = out
    ob_ref[...] = out.astype(BF16)


def _out_ln(x2d, p3, ya2d, yb2d, pj1, wo, wg, bg3, wp, lg3, lb3, layer):
    m = x2d.shape[0]
    row = lambda i: (i, 0)
    fixed = lambda i: (layer, 0, 0)
    once = pl.Buffered(1)
    return pl.pallas_call(
        _out_ln_kernel,
        grid=(m // OUT_TM,),
        in_specs=[
            pl.BlockSpec((OUT_TM, D_MODEL), row),
            pl.BlockSpec((None, OUT_TM, D_PLE), lambda i: (layer, i, 0)),
            pl.BlockSpec((OUT_TM, D_A), row),
            pl.BlockSpec((OUT_TM, D_B), row),
            pl.BlockSpec((OUT_TM, D_A), lambda i: (i, _GA_BLK)),
            pl.BlockSpec((OUT_TM, D_B), lambda i: (i, _GB_BLK)),
            pl.BlockSpec((None, D_MODEL, D_MODEL), fixed, pipeline_mode=once),
            pl.BlockSpec((None, D_MODEL, D_MODEL), fixed, pipeline_mode=once),
            pl.BlockSpec((None, 1, D_MODEL), fixed),
            pl.BlockSpec((None, D_PLE, D_MODEL), fixed, pipeline_mode=once),
            pl.BlockSpec((None, 1, D_MODEL), fixed),
            pl.BlockSpec((None, 1, D_MODEL), fixed),
        ],
        out_specs=[pl.BlockSpec((OUT_TM, D_MODEL), row), pl.BlockSpec((OUT_TM, D_MODEL), row)],
        out_shape=[jax.ShapeDtypeStruct((m, D_MODEL), F32), jax.ShapeDtypeStruct((m, D_MODEL), BF16)],
        compiler_params=pltpu.CompilerParams(
            dimension_semantics=("parallel",), vmem_limit_bytes=_VMEM_LIMIT),
        name="out_ln",
    )(x2d, p3, ya2d, yb2d, pj1, pj1, wo, wg, bg3, wp, lg3, lb3)


def _rope_col_scale():
    sc = np.ones((1, D_P2), np.float32)
    sc[:, :D_B] = Q_SCALE
    qi0 = D_B + 2 * HEAD_DIM
    sc[:, qi0:qi0 + H_IDX * D_IDX] = QI_SCALE
    wi0 = qi0 + H_IDX * D_IDX + D_IDX
    sc[:, wi0:wi0 + H_IDX] = WI_SCALE
    return jnp.asarray(sc)


def _rope_tables(positions):
    def cos_sin(dim):
        inv = ROPE_THETA ** (-jnp.arange(0, dim, 2, dtype=F32) / dim)
        ang = positions.astype(F32)[..., None] * inv
        ang = jnp.concatenate([ang, ang], -1).reshape(-1, dim)
        return jnp.cos(ang), jnp.sin(ang)

    cos_h, sin_h = cos_sin(HEAD_DIM)
    lane = jnp.arange(LANES)
    sin_h = jnp.where(lane < HEAD_DIM // 2, -sin_h, sin_h)
    cos_i, sin_i = cos_sin(D_IDX)
    cos_i = jnp.concatenate([cos_i, cos_i], -1)
    sin_i = jnp.concatenate([sin_i, sin_i], -1)
    first_half = (lane % D_IDX) < D_IDX // 2
    sa_i = jnp.where(first_half, -sin_i, 0.0)
    sb_i = jnp.where(first_half, 0.0, sin_i)
    is_ki = lane < D_IDX
    cos_l = jnp.where(is_ki, cos_i, 1.0)
    sa_l = jnp.where(is_ki, sa_i, 0.0)
    sb_l = jnp.where(is_ki, sb_i, 0.0)
    return jnp.concatenate([cos_h, sin_h, cos_i, sa_i, sb_i, cos_l, sa_l, sb_l], axis=-1)


def _band_bias_diagonals(rel_bias):
    lead = rel_bias.shape[:-1]
    n_far = 3 * A_TQ - 1 - REL_CLIP
    n_near = A_DIAG - n_far - (2 * REL_CLIP + 1)
    return jnp.concatenate([
        jnp.broadcast_to(rel_bias[..., -1:], lead + (n_far,)),
        rel_bias[..., ::-1],
        jnp.broadcast_to(rel_bias[..., :1], lead + (n_near,))], axis=-1) * LOG2E


def kernel(x, p, positions, w_in, b_in, rel_bias, w_out, w_ple, w_ple_gate, b_ple_gate, ln_g, ln_b):
    b, s, d = x.shape
    m = b * s
    assert d == D_MODEL and s % B_KSTEP == 0 and m % P1_TM == 0
    topk = min(TOPK_MAX, s // 4)
    tabs = _rope_tables(positions)
    col_scale = _rope_col_scale()
    diag = _band_bias_diagonals(rel_bias)
    w_in_t = jnp.swapaxes(w_in, 1, 2)
    b_in3 = b_in[:, None, :]
    wo, wg, wp = w_out.astype(BF16), w_ple_gate.astype(BF16), w_ple.astype(BF16)
    bg3, lg3, lb3 = b_ple_gate[:, None, :], ln_g[:, None, :], ln_b[:, None, :]
    p3 = p.reshape(DEPTH, m, D_PLE)
    x2d = x.reshape(m, d)
    xb = x2d.astype(BF16)
    for i in range(DEPTH):
        pj1 = _proj_plain(xb, w_in_t, b_in3, i)
        pj2 = _proj_rope(xb, w_in_t, b_in3, col_scale, tabs, i)
        ya = _attn_a(pj1.reshape(b, s, D_P1), diag, i)
        yb = _attn_b(pj2.reshape(b, s, D_P2), topk)
        x2d, xb = _out_ln(x2d, p3, ya.reshape(m, D_A), yb.reshape(m, D_B), pj1,
                          wo, wg, bg3, wp, lg3, lb3, i)
    return x2d.reshape(b, s, d)
```

```python
import functools

import numpy as np
import jax
import jax.numpy as jnp
from jax import lax
from jax.experimental import pallas as pl
from jax.experimental.pallas import tpu as pltpu

D_MODEL = 2048
DEPTH = 4
CHUNK = 64
LEFT_CHUNKS = 8
HEAD_DIM = 128
D_A = 1024
D_B = 1024
H_A = 8
H_B = 8
REL_CLIP = 128
H_IDX = 8
D_IDX = 64
TOPK_MAX = 256
D_PLE = 256
ROPE_THETA = 10000.0
LN_EPS = 1e-5
NEG = -1e30
ALPHA = (2.0 * DEPTH) ** 0.25
LOG2E = 1.4426950408889634

F32 = jnp.float32
BF16 = jnp.bfloat16
LANES = 128

D_IN = 6984
W_BLK = 256
Q_SCALE = HEAD_DIM ** -0.5 * LOG2E
QI_SCALE = D_IDX ** -0.5
WI_SCALE = H_IDX ** -0.5

D_P1 = 5120
_QA_BLK, _KA_BLK, _VA_BLK, _GA_BLK, _GB_BLK = 0, 1, 2, 3, 4
_P1_SPLIT = 4096 // W_BLK
_P1_SKIP = (5376 - 4096) // W_BLK
_P2_SRC = (16, 17, 18, 19, 20, 25, 26, 27)
D_P2 = 7 * W_BLK + LANES
_P2_TAIL_VALID = D_IN - 27 * W_BLK
_QB_BLK = 0
_KB_BLK, _VB_BLK, _KI_BLK = 8, 9, 14
_QI_BLK = 2
_WI_OFF = H_IDX * D_IDX + D_IDX
P_SUB = 640
_P2_KINDS = ('hhhhh', 'hhhhp', 'iiiil')

P1_TM, P1_TN = 2048, 2 * W_BLK
P2_TM = 512
A_TQ = 256
A_DIAG = 4 * A_TQ
A_HG = 2
B_TQ = 256
B_KSTEP = 256
_SEL_ROUNDS, _SEL_PASSES = 8, 4
OUT_TM = 256

_VMEM_LIMIT = 56 * 1024 * 1024


def _nt_dot(a, b):
    return lax.dot_general(a, b, (((1,), (1,)), ((), ())), preferred_element_type=F32)


def _proj_plain_kernel(x_ref, wa_ref, wb_ref, ba_ref, bb_ref, o_ref):
    j = pl.program_id(1)
    w = jnp.concatenate([wa_ref[...].astype(BF16), wb_ref[...].astype(BF16)], axis=0)
    bias = jnp.concatenate([ba_ref[...], bb_ref[...]], axis=1)
    scale = jnp.where(j < D_A // P1_TN, Q_SCALE, 1.0)
    half = P1_TM // 2
    for r in range(2):
        rows = slice(r * half, (r + 1) * half)
        acc = _nt_dot(x_ref[rows, :], w) + bias
        o_ref[rows, :] = (acc * scale).astype(BF16)


def _proj_plain(xb, w_in_t, b_in3, layer):
    m = xb.shape[0]

    def src(u):
        return u + jnp.where(u >= _P1_SPLIT, _P1_SKIP, 0)

    def wspec(k):
        return pl.BlockSpec((None, W_BLK, D_MODEL), lambda i, j: (layer, src(2 * j + k), 0))

    def bspec(k):
        return pl.BlockSpec((None, 1, W_BLK), lambda i, j: (layer, 0, src(2 * j + k)))

    return pl.pallas_call(
        _proj_plain_kernel,
        grid=(m // P1_TM, D_P1 // P1_TN),
        in_specs=[pl.BlockSpec((P1_TM, D_MODEL), lambda i, j: (i, 0)),
                  wspec(0), wspec(1), bspec(0), bspec(1)],
        out_specs=pl.BlockSpec((P1_TM, P1_TN), lambda i, j: (i, j)),
        out_shape=jax.ShapeDtypeStruct((m, D_P1), BF16),
        compiler_params=pltpu.CompilerParams(
            dimension_semantics=("parallel", "arbitrary"), vmem_limit_bytes=_VMEM_LIMIT),
        name="proj_plain",
    )(xb, w_in_t, w_in_t, b_in3, b_in3)


def _proj_rope_kernel(x_ref, *refs):
    nsrc = len(_P2_SRC)
    w_refs, b_refs = refs[:nsrc], refs[nsrc:2 * nsrc]
    scale_ref, tab_ref, o_ref, wbf_ref, bias_ref = refs[2 * nsrc:]

    @pl.when(pl.program_id(0) == 0)
    def _():
        for n in range(nsrc):
            width = W_BLK if n < nsrc - 1 else LANES
            cols = slice(n * W_BLK, n * W_BLK + width)
            w = w_refs[n][:width, :]
            b = b_refs[n][:, :width]
            if n == nsrc - 1:
                w = jnp.where(lax.broadcasted_iota(jnp.int32, (width, 1), 0) < _P2_TAIL_VALID, w, 0.0)
                b = jnp.where(lax.broadcasted_iota(jnp.int32, (1, width), 1) < _P2_TAIL_VALID, b, 0.0)
            wbf_ref[cols, :] = w.astype(BF16)
            bias_ref[:, cols] = b * scale_ref[:, cols]

    x = x_ref[...]

    def tab(k):
        return tab_ref[:, k * LANES:(k + 1) * LANES]

    lane = lax.broadcasted_iota(jnp.int32, (1, LANES), 1)
    low_half = (lane & (D_IDX // 2)) == 0

    def rope_head(t):
        return t * tab(0) + pltpu.roll(t, 64, 1) * tab(1)

    def rope_idx(t):
        rot = jnp.where(low_half, -pltpu.roll(t, LANES - D_IDX // 2, 1), pltpu.roll(t, D_IDX // 2, 1))
        return t * tab(2) + rot * tab(3)

    for s, kinds in enumerate(_P2_KINDS):
        cols = slice(s * P_SUB, (s + 1) * P_SUB)
        acc = _nt_dot(x, wbf_ref[cols, :]) * scale_ref[:, cols] + bias_ref[:, cols]
        for k, kind in enumerate(kinds):
            t = acc[:, k * LANES:(k + 1) * LANES]
            if kind == 'h':
                t = rope_head(t)
            elif kind == 'i':
                t = rope_idx(t)
            elif kind == 'l':
                t = jnp.where(lane < D_IDX, rope_idx(t), t)
            c0 = s * P_SUB + k * LANES
            o_ref[:, c0:c0 + LANES] = t.astype(BF16)


def _proj_rope(xb, w_in_t, b_in3, col_scale, tabs, layer):
    m = xb.shape[0]
    once = pl.Buffered(1)
    wspecs = [pl.BlockSpec((None, W_BLK, D_MODEL), lambda i, u=u: (layer, u, 0), pipeline_mode=once)
              for u in _P2_SRC]
    bspecs = [pl.BlockSpec((None, 1, W_BLK), lambda i, u=u: (layer, 0, u)) for u in _P2_SRC]
    nsrc = len(_P2_SRC)
    return pl.pallas_call(
        _proj_rope_kernel,
        grid=(m // P2_TM,),
        in_specs=[pl.BlockSpec((P2_TM, D_MODEL), lambda i: (i, 0))] + wspecs + bspecs + [
            pl.BlockSpec((1, D_P2), lambda i: (0, 0)),
            pl.BlockSpec((P2_TM, 4 * LANES), lambda i: (i, 0)),
        ],
        out_specs=pl.BlockSpec((P2_TM, D_P2), lambda i: (i, 0)),
        out_shape=jax.ShapeDtypeStruct((m, D_P2), BF16),
        scratch_shapes=[pltpu.VMEM((D_P2, D_MODEL), BF16), pltpu.VMEM((1, D_P2), F32)],
        compiler_params=pltpu.CompilerParams(
            dimension_semantics=("arbitrary",), vmem_limit_bytes=_VMEM_LIMIT),
        name="proj_rope",
    )(xb, *([w_in_t] * nsrc), *([b_in3] * nsrc), col_scale, tabs)


def _attn_a_kernel(q_ref, k0_ref, k1_ref, k2_ref, v0_ref, v1_ref, v2_ref, diag_ref, o_ref, bias_ref):
    i = pl.program_id(1)
    nk = 3 * A_TQ

    @pl.when((pl.program_id(0) == 0) & (i == 0))
    def _():
        qc = lax.shift_right_logical(lax.broadcasted_iota(jnp.int32, (A_TQ, nk), 0), 6)
        kc = lax.shift_right_logical(lax.broadcasted_iota(jnp.int32, (A_TQ, nk), 1), 6)
        band = (kc >= qc) & (kc <= qc + LEFT_CHUNKS)
        for h in range(H_A):
            rows = jnp.broadcast_to(diag_ref[h:h + 1, :], (A_TQ, A_DIAG))
            t = pltpu.roll(rows, A_DIAG - (A_TQ - 1), 1, stride=1, stride_axis=0)
            bias_ref[h] = jnp.where(band, t[:, :nk], NEG)

    def head_cols(h):
        return slice(h * HEAD_DIM, (h + 1) * HEAD_DIM)

    def tile(mask_left):
        for h0 in range(0, H_A, A_HG):
            heads = range(h0, h0 + A_HG)
            logits = []
            for h in heads:
                cols = head_cols(h)
                k = jnp.concatenate([k0_ref[0, :, cols], k1_ref[0, :, cols], k2_ref[0, :, cols]], axis=0)
                logits.append(_nt_dot(q_ref[0, :, cols], k) + bias_ref[h])
            s = jnp.concatenate(logits, axis=0)
            if mask_left:
                kpos = (i - 2) * A_TQ + lax.broadcasted_iota(jnp.int32, s.shape, 1)
                s = jnp.where(kpos >= 0, s, NEG)
            m = jnp.max(s, axis=-1, keepdims=True)
            p = jnp.exp2(s - m)
            l = jnp.sum(p, axis=-1, keepdims=True)
            pb = p.astype(BF16)
            for j, h in enumerate(heads):
                cols = head_cols(h)
                rows = slice(j * A_TQ, (j + 1) * A_TQ)
                v = jnp.concatenate([v0_ref[0, :, cols], v1_ref[0, :, cols], v2_ref[0, :, cols]], axis=0)
                o = jnp.dot(pb[rows], v, preferred_element_type=F32) / l[rows]
                o_ref[0, :, cols] = o.astype(BF16)

    @pl.when(i < 2)
    def _():
        tile(True)

    @pl.when(i >= 2)
    def _():
        tile(False)


def _attn_a(pj1, diag, layer):
    b, s, _ = pj1.shape
    blk = (1, A_TQ, D_A)

    def kv_spec(col, back):
        return pl.BlockSpec(blk, lambda bb, i: (bb, jnp.maximum(i - back, 0), col))

    return pl.pallas_call(
        _attn_a_kernel,
        grid=(b, s // A_TQ),
        in_specs=[
            pl.BlockSpec(blk, lambda bb, i: (bb, i, _QA_BLK)),
            kv_spec(_KA_BLK, 2), kv_spec(_KA_BLK, 1), kv_spec(_KA_BLK, 0),
            kv_spec(_VA_BLK, 2), kv_spec(_VA_BLK, 1), kv_spec(_VA_BLK, 0),
            pl.BlockSpec((None, H_A, A_DIAG), lambda bb, i: (layer, 0, 0)),
        ],
        out_specs=pl.BlockSpec(blk, lambda bb, i: (bb, i, 0)),
        out_shape=jax.ShapeDtypeStruct((b, s, D_A), BF16),
        scratch_shapes=[pltpu.VMEM((H_A, A_TQ, 3 * A_TQ), F32)],
        compiler_params=pltpu.CompilerParams(
            dimension_semantics=("arbitrary", "arbitrary"), vmem_limit_bytes=_VMEM_LIMIT),
        name="attn_a",
    )(pj1, pj1, pj1, pj1, pj1, pj1, pj1, diag)


def _float_key(v):
    bits = int(np.float32(v).view(np.int32))
    return bits if bits >= 0 else bits ^ 0x7FFFFFFF


_KEY_LO = _float_key(NEG)
_KEY_HI = 0x7F800001


def _key_to_float(k):
    bits = jnp.where(k >= 0, k, k ^ 0x7FFFFFFF)
    return lax.bitcast_convert_type(bits, F32)


def _topk_mask_bias(score_ref, bias_ref, nk, topk, side_work):
    tq = score_ref.shape[0]
    nt = nk // LANES
    shape = (tq, LANES)
    row_groups = [slice(g * LANES, (g + 1) * LANES) for g in range(tq // LANES)]

    def tile(rows, t):
        return score_ref[rows, t * LANES:(t + 1) * LANES]

    def row_reduce(tile_fn, combine, lane_reduce):
        out = []
        for rows in row_groups:
            acc = tile_fn(tile(rows, 0), 0, rows)
            for t in range(1, nt):
                acc = combine(acc, tile_fn(tile(rows, t), t, rows))
            out.append(jnp.broadcast_to(lane_reduce(acc, axis=1, keepdims=True), (LANES, LANES)))
        return jnp.concatenate(out, axis=0)

    def count(pred):
        return row_reduce(lambda sc, t, rows: jnp.where(pred(sc, t, rows), 1.0, 0.0),
                          jnp.add, jnp.sum)

    def step(lo, hi, c_lo, mid):
        th = _key_to_float(mid)
        c = count(lambda sc, t, rows: sc >= th[rows])
        ge = c >= topk
        return jnp.where(ge, mid, lo), jnp.where(ge, hi, mid), jnp.where(ge, c, c_lo)

    def bisect_round(r, carry):
        lo, hi, c_lo = carry
        for _ in range(_SEL_PASSES):
            lo, hi, c_lo = step(lo, hi, c_lo, lo + lax.shift_right_logical(hi - lo, 1))
        side_work(r)
        return lo, hi, c_lo

    lo, _, c_lo = lax.fori_loop(
        0, _SEL_ROUNDS, bisect_round,
        (jnp.full(shape, _KEY_LO, jnp.int32), jnp.full(shape, _KEY_HI, jnp.int32),
         jnp.full(shape, float(nk), F32)))
    thr = _key_to_float(lo)
    partial = jnp.where((c_lo > topk) & (thr > 0.5 * NEG), 1.0, 0.0)
    any_partial = jnp.max(jnp.max(partial, axis=1, keepdims=True), axis=0, keepdims=True)[0, 0] > 0.0

    @pl.when(jnp.logical_not(any_partial))
    def _():
        for rows in row_groups:
            for t in range(nt):
                sc = tile(rows, t)
                sel = (sc >= thr[rows]) & (sc > 0.5 * NEG)
                bias_ref[rows, t * LANES:(t + 1) * LANES] = jnp.where(sel, 0.0, NEG)

    @pl.when(any_partial)
    def _():
        need = topk - count(lambda sc, t, rows: sc > thr[rows])
        lane = lax.broadcasted_iota(jnp.int32, (LANES, LANES), 1)

        def tie_bisect(b, jsel):
            cand = jsel + jnp.left_shift(jnp.int32(1), 10 - b)
            c = count(lambda sc, t, rows: (sc == thr[rows]) & (lane + t * LANES < cand[rows]))
            return jnp.where(c < need, cand, jsel)

        jsel = lax.fori_loop(0, 11, tie_bisect, jnp.zeros(shape, jnp.int32))
        for rows in row_groups:
            for t in range(nt):
                sc = tile(rows, t)
                sel = (sc > thr[rows]) | ((sc == thr[rows]) & (lane + t * LANES <= jsel[rows]))
                sel = sel & (sc > 0.5 * NEG)
                bias_ref[rows, t * LANES:(t + 1) * LANES] = jnp.where(sel, 0.0, NEG)


def _attn_b_block(nk, i, qb_ref, kb_ref, vb_ref, qiw_ref, kiw_ref, o_ref, score_ref, bias_ref,
                  s_ref, qh_ref, topk):
    tq = score_ref.shape[0]
    ki = kiw_ref[0, :nk, :D_IDX]
    vb = vb_ref[0, :nk, :]
    wi = qiw_ref[0, :, _WI_OFF:_WI_OFF + H_IDX].astype(F32)

    score = jnp.zeros((tq, nk), F32)
    for h in range(H_IDX):
        qh = qiw_ref[0, :, h * D_IDX:(h + 1) * D_IDX]
        score = score + wi[:, h:h + 1] * jnp.maximum(_nt_dot(qh, ki), 0.0)
    t_pos = i * tq + lax.broadcasted_iota(jnp.int32, (tq, nk), 0)
    visible_end = (t_pos // CHUNK + 1) * CHUNK
    key_pos = lax.broadcasted_iota(jnp.int32, (tq, nk), 1)
    score_ref[:, :nk] = jnp.where(key_pos < visible_end, score, NEG)
    for h in range(H_B):
        qh_ref[h] = qb_ref[0, :, h * HEAD_DIM:(h + 1) * HEAD_DIM]

    def qk_logits(r):
        for j in range(H_B // _SEL_ROUNDS):
            h = r * (H_B // _SEL_ROUNDS) + j
            s_ref[h, :, :nk] = _nt_dot(qh_ref[h], kb_ref[0, :nk, :])

    _topk_mask_bias(score_ref, bias_ref, nk, topk, qk_logits)

    for h in range(H_B):
        sc = s_ref[h, :, :nk] + bias_ref[:, :nk]
        m = jnp.max(sc, axis=-1, keepdims=True)
        p = jnp.exp2(sc - m)
        l = jnp.sum(p, axis=-1, keepdims=True)
        o = jnp.dot(p.astype(BF16), vb, preferred_element_type=F32) / l
        o_ref[0, :, h * HEAD_DIM:(h + 1) * HEAD_DIM] = o.astype(BF16)


def _attn_b_kernel(qb_ref, kb_ref, vb_ref, qiw_ref, kiw_ref, o_ref, score_ref, bias_ref, s_ref,
                   qh_ref, *, topk):
    i = pl.program_id(1)
    tq, s = score_ref.shape
    per = B_KSTEP // tq
    for n in range(1, s // B_KSTEP + 1):
        @pl.when(i // per == n - 1)
        def _(n=n):
            _attn_b_block(n * B_KSTEP, i, qb_ref, kb_ref, vb_ref, qiw_ref, kiw_ref, o_ref,
                          score_ref, bias_ref, s_ref, qh_ref, topk)


def _attn_b(pj2, topk):
    b, s, _ = pj2.shape
    return pl.pallas_call(
        functools.partial(_attn_b_kernel, topk=topk),
        grid=(b, s // B_TQ),
        in_specs=[
            pl.BlockSpec((1, B_TQ, D_B), lambda bb, i: (bb, i, _QB_BLK)),
            pl.BlockSpec((1, s, HEAD_DIM), lambda bb, i: (bb, 0, _KB_BLK)),
            pl.BlockSpec((1, s, HEAD_DIM), lambda bb, i: (bb, 0, _VB_BLK)),
            pl.BlockSpec((1, B_TQ, P_SUB), lambda bb, i: (bb, i, _QI_BLK)),
            pl.BlockSpec((1, s, LANES), lambda bb, i: (bb, 0, _KI_BLK)),
        ],
        out_specs=pl.BlockSpec((1, B_TQ, D_B), lambda bb, i: (bb, i, 0)),
        out_shape=jax.ShapeDtypeStruct((b, s, D_B), BF16),
        scratch_shapes=[pltpu.VMEM((B_TQ, s), F32), pltpu.VMEM((B_TQ, s), F32),
                        pltpu.VMEM((H_B, B_TQ, s), F32), pltpu.VMEM((H_B, B_TQ, HEAD_DIM), BF16)],
        compiler_params=pltpu.CompilerParams(
            dimension_semantics=("parallel", "parallel"), vmem_limit_bytes=_VMEM_LIMIT),
        name="attn_b",
    )(pj2, pj2, pj2, pj2, pj2)


def _sigmoid(v):
    return 1.0 / (1.0 + jnp.exp(-v))


def _out_ln_kernel(x_ref, p_ref, ya_ref, yb_ref, ga_ref, gb_ref, wo_ref, wg_ref, bg_ref, wp_ref,
                   lg_ref, lb_ref, o_ref, ob_ref):
    x = x_ref[...]
    ga = ga_ref[...].astype(F32)
    gb = gb_ref[...].astype(F32)
    ua = (ya_ref[...].astype(F32) * (ga * _sigmoid(ga))).astype(BF16)
    ub = (yb_ref[...].astype(F32) * (gb * _sigmoid(gb))).astype(BF16)
    y = (jnp.dot(ua, wo_ref[:D_A, :], preferred_element_type=F32)
         + jnp.dot(ub, wo_ref[D_A:, :], preferred_element_type=F32))
    gate = _sigmoid(jnp.dot(x.astype(BF16), wg_ref[...], preferred_element_type=F32) + bg_ref[...])
    ple = gate * jnp.dot(p_ref[...].astype(BF16), wp_ref[...], preferred_element_type=F32)
    z = ALPHA * x + y + ple
    mu = jnp.mean(z, axis=-1, keepdims=True)
    zc = z - mu
    var = jnp.mean(zc * zc, axis=-1, keepdims=True)
    out = zc * lax.rsqrt(var + LN_EPS) * lg_ref[...] + lb_ref[...]
    o_ref[...] = out
    ob_ref[...] = out.astype(BF16)


def _out_ln(x2d, p3, ya2d, yb2d, pj1, wo, wg, bg3, wp, lg3, lb3, layer):
    m = x2d.shape[0]
    row = lambda i: (i, 0)
    fixed = lambda i: (layer, 0, 0)
    once = pl.Buffered(1)
    return pl.pallas_call(
        _out_ln_kernel,
        grid=(m // OUT_TM,),
        in_specs=[
            pl.BlockSpec((OUT_TM, D_MODEL), row),
            pl.BlockSpec((None, OUT_TM, D_PLE), lambda i: (layer, i, 0)),
            pl.BlockSpec((OUT_TM, D_A), row),
            pl.BlockSpec((OUT_TM, D_B), row),
            pl.BlockSpec((OUT_TM, D_A), lambda i: (i, _GA_BLK)),
            pl.BlockSpec((OUT_TM, D_B), lambda i: (i, _GB_BLK)),
            pl.BlockSpec((None, D_MODEL, D_MODEL), fixed, pipeline_mode=once),
            pl.BlockSpec((None, D_MODEL, D_MODEL), fixed, pipeline_mode=once),
            pl.BlockSpec((None, 1, D_MODEL), fixed),
            pl.BlockSpec((None, D_PLE, D_MODEL), fixed, pipeline_mode=once),
            pl.BlockSpec((None, 1, D_MODEL), fixed),
            pl.BlockSpec((None, 1, D_MODEL), fixed),
        ],
        out_specs=[pl.BlockSpec((OUT_TM, D_MODEL), row), pl.BlockSpec((OUT_TM, D_MODEL), row)],
        out_shape=[jax.ShapeDtypeStruct((m, D_MODEL), F32), jax.ShapeDtypeStruct((m, D_MODEL), BF16)],
        compiler_params=pltpu.CompilerParams(
            dimension_semantics=("parallel",), vmem_limit_bytes=_VMEM_LIMIT),
        name="out_ln",
    )(x2d, p3, ya2d, yb2d, pj1, pj1, wo, wg, bg3, wp, lg3, lb3)


def _rope_col_scale():
    sc = np.ones((1, D_P2), np.float32)
    sc[:, :D_B] = Q_SCALE
    qi0 = D_B + 2 * HEAD_DIM
    sc[:, qi0:qi0 + H_IDX * D_IDX] = QI_SCALE
    wi0 = qi0 + H_IDX * D_IDX + D_IDX
    sc[:, wi0:wi0 + H_IDX] = WI_SCALE
    return jnp.asarray(sc)


def _rope_tables(positions):
    def cos_sin(dim):
        inv = ROPE_THETA ** (-jnp.arange(0, dim, 2, dtype=F32) / dim)
        ang = positions.astype(F32)[..., None] * inv
        ang = jnp.concatenate([ang, ang], -1).reshape(-1, dim)
        return jnp.cos(ang), jnp.sin(ang)

    cos_h, sin_h = cos_sin(HEAD_DIM)
    sin_h = jnp.where(jnp.arange(LANES) < HEAD_DIM // 2, -sin_h, sin_h)
    cos_i, sin_i = cos_sin(D_IDX)
    return jnp.concatenate([cos_h, sin_h, cos_i, cos_i, sin_i, sin_i], axis=-1)


def _band_bias_diagonals(rel_bias):
    lead = rel_bias.shape[:-1]
    n_far = 3 * A_TQ - 1 - REL_CLIP
    n_near = A_DIAG - n_far - (2 * REL_CLIP + 1)
    return jnp.concatenate([
        jnp.broadcast_to(rel_bias[..., -1:], lead + (n_far,)),
        rel_bias[..., ::-1],
        jnp.broadcast_to(rel_bias[..., :1], lead + (n_near,))], axis=-1) * LOG2E


def kernel(x, p, positions, w_in, b_in, rel_bias, w_out, w_ple, w_ple_gate, b_ple_gate, ln_g, ln_b):
    b, s, d = x.shape
    m = b * s
    assert d == D_MODEL and s % B_KSTEP == 0 and m % P1_TM == 0
    topk = min(TOPK_MAX, s // 4)
    tabs = _rope_tables(positions)
    col_scale = _rope_col_scale()
    diag = _band_bias_diagonals(rel_bias)
    w_in_t = jnp.swapaxes(w_in, 1, 2)
    b_in3 = b_in[:, None, :]
    wo, wg, wp = w_out.astype(BF16), w_ple_gate.astype(BF16), w_ple.astype(BF16)
    bg3, lg3, lb3 = b_ple_gate[:, None, :], ln_g[:, None, :], ln_b[:, None, :]
    p3 = p.reshape(DEPTH, m, D_PLE)
    x2d = x.reshape(m, d)
    xb = x2d.astype(BF16)
    for i in range(DEPTH):
        pj1 = _proj_plain(xb, w_in_t, b_in3, i)
        pj2 = _proj_rope(xb, w_in_t, b_in3, col_scale, tabs, i)
        ya = _attn_a(pj1.reshape(b, s, D_P1), diag, i)
        yb = _attn_b(pj2.reshape(b, s, D_P2), topk)
        x2d, xb = _out_ln(x2d, p3, ya.reshape(m, D_A), yb.reshape(m, D_B), pj1,
                          wo, wg, bg3, wp, lg3, lb3, i)
    return x2d.reshape(b, s, d)
```

```python
import functools

import numpy as np
import jax
import jax.numpy as jnp
from jax import lax
from jax.experimental import pallas as pl
from jax.experimental.pallas import tpu as pltpu

D_MODEL = 2048
DEPTH = 4
CHUNK = 64
LEFT_CHUNKS = 8
HEAD_DIM = 128
D_A = 1024
D_B = 1024
H_A = 8
H_B = 8
REL_CLIP = 128
H_IDX = 8
D_IDX = 64
TOPK_MAX = 256
D_PLE = 256
ROPE_THETA = 10000.0
LN_EPS = 1e-5
NEG = -1e30
ALPHA = (2.0 * DEPTH) ** 0.25
LOG2E = 1.4426950408889634

F32 = jnp.float32
BF16 = jnp.bfloat16
LANES = 128

D_IN = 6984
W_BLK = 256
Q_SCALE = HEAD_DIM ** -0.5 * LOG2E
QI_SCALE = D_IDX ** -0.5
WI_SCALE = H_IDX ** -0.5

D_P1 = 5120
_QA_BLK, _KA_BLK, _VA_BLK, _GA_BLK, _GB_BLK = 0, 1, 2, 3, 4
_P1_SPLIT = 4096 // W_BLK
_P1_SKIP = (5376 - 4096) // W_BLK
_P2_SRC = (16, 17, 18, 19, 20, 25, 26, 27)
D_P2 = 7 * W_BLK + LANES
_P2_TAIL_VALID = D_IN - 27 * W_BLK
_QB_BLK = 0
_KB_BLK, _VB_BLK, _KI_BLK = 8, 9, 14
_QI_BLK = 2
_WI_OFF = H_IDX * D_IDX + D_IDX
P_SUB = 640
_P2_KINDS = ('hhhhh', 'hhhhp', 'iiiil')

P1_TM, P1_TN = 2048, 2 * W_BLK
P2_TM = 512
A_TQ = 256
A_DIAG = 4 * A_TQ
A_HG = 2
B_TQ = 256
B_KSTEP = 512
_SEL_ROUNDS, _SEL_PASSES = 8, 4
OUT_TM = 256

_VMEM_LIMIT = 56 * 1024 * 1024


def _nt_dot(a, b):
    return lax.dot_general(a, b, (((1,), (1,)), ((), ())), preferred_element_type=F32)


def _proj_plain_kernel(x_ref, wa_ref, wb_ref, ba_ref, bb_ref, o_ref):
    j = pl.program_id(1)
    w = jnp.concatenate([wa_ref[...].astype(BF16), wb_ref[...].astype(BF16)], axis=0)
    bias = jnp.concatenate([ba_ref[...], bb_ref[...]], axis=1)
    scale = jnp.where(j < D_A // P1_TN, Q_SCALE, 1.0)
    half = P1_TM // 2
    for r in range(2):
        rows = slice(r * half, (r + 1) * half)
        acc = _nt_dot(x_ref[rows, :], w) + bias
        o_ref[rows, :] = (acc * scale).astype(BF16)


def _proj_plain(xb, w_in_t, b_in3, layer):
    m = xb.shape[0]

    def src(u):
        return u + jnp.where(u >= _P1_SPLIT, _P1_SKIP, 0)

    def wspec(k):
        return pl.BlockSpec((None, W_BLK, D_MODEL), lambda i, j: (layer, src(2 * j + k), 0))

    def bspec(k):
        return pl.BlockSpec((None, 1, W_BLK), lambda i, j: (layer, 0, src(2 * j + k)))

    return pl.pallas_call(
        _proj_plain_kernel,
        grid=(m // P1_TM, D_P1 // P1_TN),
        in_specs=[pl.BlockSpec((P1_TM, D_MODEL), lambda i, j: (i, 0)),
                  wspec(0), wspec(1), bspec(0), bspec(1)],
        out_specs=pl.BlockSpec((P1_TM, P1_TN), lambda i, j: (i, j)),
        out_shape=jax.ShapeDtypeStruct((m, D_P1), BF16),
        compiler_params=pltpu.CompilerParams(
            dimension_semantics=("parallel", "arbitrary"), vmem_limit_bytes=_VMEM_LIMIT),
        name="proj_plain",
    )(xb, w_in_t, w_in_t, b_in3, b_in3)


def _proj_rope_kernel(x_ref, *refs):
    nsrc = len(_P2_SRC)
    w_refs, b_refs = refs[:nsrc], refs[nsrc:2 * nsrc]
    scale_ref, tab_ref, o_ref, wbf_ref, bias_ref = refs[2 * nsrc:]

    @pl.when(pl.program_id(0) == 0)
    def _():
        for n in range(nsrc):
            width = W_BLK if n < nsrc - 1 else LANES
            cols = slice(n * W_BLK, n * W_BLK + width)
            w = w_refs[n][:width, :]
            b = b_refs[n][:, :width]
            if n == nsrc - 1:
                w = jnp.where(lax.broadcasted_iota(jnp.int32, (width, 1), 0) < _P2_TAIL_VALID, w, 0.0)
                b = jnp.where(lax.broadcasted_iota(jnp.int32, (1, width), 1) < _P2_TAIL_VALID, b, 0.0)
            wbf_ref[cols, :] = w.astype(BF16)
            bias_ref[:, cols] = b * scale_ref[:, cols]

    x = x_ref[...]

    def tab(k):
        return tab_ref[:, k * LANES:(k + 1) * LANES]

    lane = lax.broadcasted_iota(jnp.int32, (1, LANES), 1)
    low_half = (lane & (D_IDX // 2)) == 0

    def rope_head(t):
        return t * tab(0) + pltpu.roll(t, 64, 1) * tab(1)

    def rope_idx(t):
        rot = jnp.where(low_half, -pltpu.roll(t, LANES - D_IDX // 2, 1), pltpu.roll(t, D_IDX // 2, 1))
        return t * tab(2) + rot * tab(3)

    for s, kinds in enumerate(_P2_KINDS):
        cols = slice(s * P_SUB, (s + 1) * P_SUB)
        acc = _nt_dot(x, wbf_ref[cols, :]) * scale_ref[:, cols] + bias_ref[:, cols]
        for k, kind in enumerate(kinds):
            t = acc[:, k * LANES:(k + 1) * LANES]
            if kind == 'h':
                t = rope_head(t)
            elif kind == 'i':
                t = rope_idx(t)
            elif kind == 'l':
                t = jnp.where(lane < D_IDX, rope_idx(t), t)
            c0 = s * P_SUB + k * LANES
            o_ref[:, c0:c0 + LANES] = t.astype(BF16)


def _proj_rope(xb, w_in_t, b_in3, col_scale, tabs, layer):
    m = xb.shape[0]
    once = pl.Buffered(1)
    wspecs = [pl.BlockSpec((None, W_BLK, D_MODEL), lambda i, u=u: (layer, u, 0), pipeline_mode=once)
              for u in _P2_SRC]
    bspecs = [pl.BlockSpec((None, 1, W_BLK), lambda i, u=u: (layer, 0, u)) for u in _P2_SRC]
    nsrc = len(_P2_SRC)
    return pl.pallas_call(
        _proj_rope_kernel,
        grid=(m // P2_TM,),
        in_specs=[pl.BlockSpec((P2_TM, D_MODEL), lambda i: (i, 0))] + wspecs + bspecs + [
            pl.BlockSpec((1, D_P2), lambda i: (0, 0)),
            pl.BlockSpec((P2_TM, 4 * LANES), lambda i: (i, 0)),
        ],
        out_specs=pl.BlockSpec((P2_TM, D_P2), lambda i: (i, 0)),
        out_shape=jax.ShapeDtypeStruct((m, D_P2), BF16),
        scratch_shapes=[pltpu.VMEM((D_P2, D_MODEL), BF16), pltpu.VMEM((1, D_P2), F32)],
        compiler_params=pltpu.CompilerParams(
            dimension_semantics=("arbitrary",), vmem_limit_bytes=_VMEM_LIMIT),
        name="proj_rope",
    )(xb, *([w_in_t] * nsrc), *([b_in3] * nsrc), col_scale, tabs)


def _attn_a_kernel(q_ref, k0_ref, k1_ref, k2_ref, v0_ref, v1_ref, v2_ref, diag_ref, o_ref, bias_ref):
    i = pl.program_id(1)
    nk = 3 * A_TQ

    @pl.when((pl.program_id(0) == 0) & (i == 0))
    def _():
        qc = lax.shift_right_logical(lax.broadcasted_iota(jnp.int32, (A_TQ, nk), 0), 6)
        kc = lax.shift_right_logical(lax.broadcasted_iota(jnp.int32, (A_TQ, nk), 1), 6)
        band = (kc >= qc) & (kc <= qc + LEFT_CHUNKS)
        for h in range(H_A):
            rows = jnp.broadcast_to(diag_ref[h:h + 1, :], (A_TQ, A_DIAG))
            t = pltpu.roll(rows, A_DIAG - (A_TQ - 1), 1, stride=1, stride_axis=0)
            bias_ref[h] = jnp.where(band, t[:, :nk], NEG)

    def head_cols(h):
        return slice(h * HEAD_DIM, (h + 1) * HEAD_DIM)

    def tile(mask_left):
        for h0 in range(0, H_A, A_HG):
            heads = range(h0, h0 + A_HG)
            logits = []
            for h in heads:
                cols = head_cols(h)
                k = jnp.concatenate([k0_ref[0, :, cols], k1_ref[0, :, cols], k2_ref[0, :, cols]], axis=0)
                logits.append(_nt_dot(q_ref[0, :, cols], k) + bias_ref[h])
            s = jnp.concatenate(logits, axis=0)
            if mask_left:
                kpos = (i - 2) * A_TQ + lax.broadcasted_iota(jnp.int32, s.shape, 1)
                s = jnp.where(kpos >= 0, s, NEG)
            m = jnp.max(s, axis=-1, keepdims=True)
            p = jnp.exp2(s - m)
            l = jnp.sum(p, axis=-1, keepdims=True)
            pb = p.astype(BF16)
            for j, h in enumerate(heads):
                cols = head_cols(h)
                rows = slice(j * A_TQ, (j + 1) * A_TQ)
                v = jnp.concatenate([v0_ref[0, :, cols], v1_ref[0, :, cols], v2_ref[0, :, cols]], axis=0)
                o = jnp.dot(pb[rows], v, preferred_element_type=F32) / l[rows]
                o_ref[0, :, cols] = o.astype(BF16)

    @pl.when(i < 2)
    def _():
        tile(True)

    @pl.when(i >= 2)
    def _():
        tile(False)


def _attn_a(pj1, diag, layer):
    b, s, _ = pj1.shape
    blk = (1, A_TQ, D_A)

    def kv_spec(col, back):
        return pl.BlockSpec(blk, lambda bb, i: (bb, jnp.maximum(i - back, 0), col))

    return pl.pallas_call(
        _attn_a_kernel,
        grid=(b, s // A_TQ),
        in_specs=[
            pl.BlockSpec(blk, lambda bb, i: (bb, i, _QA_BLK)),
            kv_spec(_KA_BLK, 2), kv_spec(_KA_BLK, 1), kv_spec(_KA_BLK, 0),
            kv_spec(_VA_BLK, 2), kv_spec(_VA_BLK, 1), kv_spec(_VA_BLK, 0),
            pl.BlockSpec((None, H_A, A_DIAG), lambda bb, i: (layer, 0, 0)),
        ],
        out_specs=pl.BlockSpec(blk, lambda bb, i: (bb, i, 0)),
        out_shape=jax.ShapeDtypeStruct((b, s, D_A), BF16),
        scratch_shapes=[pltpu.VMEM((H_A, A_TQ, 3 * A_TQ), F32)],
        compiler_params=pltpu.CompilerParams(
            dimension_semantics=("arbitrary", "arbitrary"), vmem_limit_bytes=_VMEM_LIMIT),
        name="attn_a",
    )(pj1, pj1, pj1, pj1, pj1, pj1, pj1, diag)


def _float_key(v):
    bits = int(np.float32(v).view(np.int32))
    return bits if bits >= 0 else bits ^ 0x7FFFFFFF


_KEY_LO = _float_key(NEG)
_KEY_HI = 0x7F800001


def _key_to_float(k):
    bits = jnp.where(k >= 0, k, k ^ 0x7FFFFFFF)
    return lax.bitcast_convert_type(bits, F32)


def _topk_mask_bias(score_ref, bias_ref, nk, topk, side_work):
    tq = score_ref.shape[0]
    nt = nk // LANES
    shape = (tq, LANES)
    row_groups = [slice(g * LANES, (g + 1) * LANES) for g in range(tq // LANES)]

    def tile(rows, t):
        return score_ref[rows, t * LANES:(t + 1) * LANES]

    def row_reduce(tile_fn, combine, lane_reduce):
        out = []
        for rows in row_groups:
            acc = tile_fn(tile(rows, 0), 0, rows)
            for t in range(1, nt):
                acc = combine(acc, tile_fn(tile(rows, t), t, rows))
            out.append(jnp.broadcast_to(lane_reduce(acc, axis=1, keepdims=True), (LANES, LANES)))
        return jnp.concatenate(out, axis=0)

    def count(pred):
        return row_reduce(lambda sc, t, rows: jnp.where(pred(sc, t, rows), 1.0, 0.0),
                          jnp.add, jnp.sum)

    def step(lo, hi, c_lo, mid):
        th = _key_to_float(mid)
        c = count(lambda sc, t, rows: sc >= th[rows])
        ge = c >= topk
        return jnp.where(ge, mid, lo), jnp.where(ge, hi, mid), jnp.where(ge, c, c_lo)

    def bisect_round(r, carry):
        lo, hi, c_lo = carry
        for _ in range(_SEL_PASSES):
            lo, hi, c_lo = step(lo, hi, c_lo, lo + lax.shift_right_logical(hi - lo, 1))
        side_work(r)
        return lo, hi, c_lo

    lo, _, c_lo = lax.fori_loop(
        0, _SEL_ROUNDS, bisect_round,
        (jnp.full(shape, _KEY_LO, jnp.int32), jnp.full(shape, _KEY_HI, jnp.int32),
         jnp.full(shape, float(nk), F32)))
    thr = _key_to_float(lo)
    partial = jnp.where((c_lo > topk) & (thr > 0.5 * NEG), 1.0, 0.0)
    any_partial = jnp.max(jnp.max(partial, axis=1, keepdims=True), axis=0, keepdims=True)[0, 0] > 0.0

    @pl.when(jnp.logical_not(any_partial))
    def _():
        for rows in row_groups:
            for t in range(nt):
                sc = tile(rows, t)
                sel = (sc >= thr[rows]) & (sc > 0.5 * NEG)
                bias_ref[rows, t * LANES:(t + 1) * LANES] = jnp.where(sel, 0.0, NEG)

    @pl.when(any_partial)
    def _():
        need = topk - count(lambda sc, t, rows: sc > thr[rows])
        lane = lax.broadcasted_iota(jnp.int32, (LANES, LANES), 1)

        def tie_bisect(b, jsel):
            cand = jsel + jnp.left_shift(jnp.int32(1), 10 - b)
            c = count(lambda sc, t, rows: (sc == thr[rows]) & (lane + t * LANES < cand[rows]))
            return jnp.where(c < need, cand, jsel)

        jsel = lax.fori_loop(0, 11, tie_bisect, jnp.zeros(shape, jnp.int32))
        for rows in row_groups:
            for t in range(nt):
                sc = tile(rows, t)
                sel = (sc > thr[rows]) | ((sc == thr[rows]) & (lane + t * LANES <= jsel[rows]))
                sel = sel & (sc > 0.5 * NEG)
                bias_ref[rows, t * LANES:(t + 1) * LANES] = jnp.where(sel, 0.0, NEG)


def _attn_b_block(nk, i, qb_ref, kb_ref, vb_ref, qiw_ref, kiw_ref, o_ref, score_ref, bias_ref,
                  s_ref, qh_ref, topk):
    tq = score_ref.shape[0]
    ki = kiw_ref[0, :nk, :D_IDX]
    vb = vb_ref[0, :nk, :]
    wi = qiw_ref[0, :, _WI_OFF:_WI_OFF + H_IDX].astype(F32)

    score = jnp.zeros((tq, nk), F32)
    for h in range(H_IDX):
        qh = qiw_ref[0, :, h * D_IDX:(h + 1) * D_IDX]
        score = score + wi[:, h:h + 1] * jnp.maximum(_nt_dot(qh, ki), 0.0)
    t_pos = i * tq + lax.broadcasted_iota(jnp.int32, (tq, nk), 0)
    visible_end = (t_pos // CHUNK + 1) * CHUNK
    key_pos = lax.broadcasted_iota(jnp.int32, (tq, nk), 1)
    score_ref[:, :nk] = jnp.where(key_pos < visible_end, score, NEG)
    for h in range(H_B):
        qh_ref[h] = qb_ref[0, :, h * HEAD_DIM:(h + 1) * HEAD_DIM]

    def qk_logits(r):
        for j in range(H_B // _SEL_ROUNDS):
            h = r * (H_B // _SEL_ROUNDS) + j
            s_ref[h, :, :nk] = _nt_dot(qh_ref[h], kb_ref[0, :nk, :])

    _topk_mask_bias(score_ref, bias_ref, nk, topk, qk_logits)

    for h in range(H_B):
        sc = s_ref[h, :, :nk] + bias_ref[:, :nk]
        m = jnp.max(sc, axis=-1, keepdims=True)
        p = jnp.exp2(sc - m)
        l = jnp.sum(p, axis=-1, keepdims=True)
        o = jnp.dot(p.astype(BF16), vb, preferred_element_type=F32) / l
        o_ref[0, :, h * HEAD_DIM:(h + 1) * HEAD_DIM] = o.astype(BF16)


def _attn_b_kernel(qb_ref, kb_ref, vb_ref, qiw_ref, kiw_ref, o_ref, score_ref, bias_ref, s_ref,
                   qh_ref, *, topk):
    i = pl.program_id(1)
    tq, s = score_ref.shape
    per = B_KSTEP // tq
    for n in range(1, s // B_KSTEP + 1):
        @pl.when(i // per == n - 1)
        def _(n=n):
            _attn_b_block(n * B_KSTEP, i, qb_ref, kb_ref, vb_ref, qiw_ref, kiw_ref, o_ref,
                          score_ref, bias_ref, s_ref, qh_ref, topk)


def _attn_b(pj2, topk):
    b, s, _ = pj2.shape
    return pl.pallas_call(
        functools.partial(_attn_b_kernel, topk=topk),
        grid=(b, s // B_TQ),
        in_specs=[
            pl.BlockSpec((1, B_TQ, D_B), lambda bb, i: (bb, i, _QB_BLK)),
            pl.BlockSpec((1, s, HEAD_DIM), lambda bb, i: (bb, 0, _KB_BLK)),
            pl.BlockSpec((1, s, HEAD_DIM), lambda bb, i: (bb, 0, _VB_BLK)),
            pl.BlockSpec((1, B_TQ, P_SUB), lambda bb, i: (bb, i, _QI_BLK)),
            pl.BlockSpec((1, s, LANES), lambda bb, i: (bb, 0, _KI_BLK)),
        ],
        out_specs=pl.BlockSpec((1, B_TQ, D_B), lambda bb, i: (bb, i, 0)),
        out_shape=jax.ShapeDtypeStruct((b, s, D_B), BF16),
        scratch_shapes=[pltpu.VMEM((B_TQ, s), F32), pltpu.VMEM((B_TQ, s), F32),
                        pltpu.VMEM((H_B, B_TQ, s), F32), pltpu.VMEM((H_B, B_TQ, HEAD_DIM), BF16)],
        compiler_params=pltpu.CompilerParams(
            dimension_semantics=("parallel", "parallel"), vmem_limit_bytes=_VMEM_LIMIT),
        name="attn_b",
    )(pj2, pj2, pj2, pj2, pj2)


def _sigmoid(v):
    return 1.0 / (1.0 + jnp.exp(-v))


def _out_ln_kernel(x_ref, p_ref, ya_ref, yb_ref, ga_ref, gb_ref, wo_ref, wg_ref, bg_ref, wp_ref,
                   lg_ref, lb_ref, o_ref, ob_ref):
    x = x_ref[...]
    ga = ga_ref[...].astype(F32)
    gb = gb_ref[...].astype(F32)
    ua = (ya_ref[...].astype(F32) * (ga * _sigmoid(ga))).astype(BF16)
    ub = (yb_ref[...].astype(F32) * (gb * _sigmoid(gb))).astype(BF16)
    y = (jnp.dot(ua, wo_ref[:D_A, :], preferred_element_type=F32)
         + jnp.dot(ub, wo_ref[D_A:, :], preferred_element_type=F32))
    gate = _sigmoid(jnp.dot(x.astype(BF16), wg_ref[...], preferred_element_type=F32) + bg_ref[...])
    ple = gate * jnp.dot(p_ref[...].astype(BF16), wp_ref[...], preferred_element_type=F32)
    z = ALPHA * x + y + ple
    mu = jnp.mean(z, axis=-1, keepdims=True)
    zc = z - mu
    var = jnp.mean(zc * zc, axis=-1, keepdims=True)
    out = zc * lax.rsqrt(var + LN_EPS) * lg_ref[...] + lb_ref[...]
    o_ref[...] = out
    ob_ref[...] = out.astype(BF16)


def _out_ln(x2d, p3, ya2d, yb2d, pj1, wo, wg, bg3, wp, lg3, lb3, layer):
    m = x2d.shape[0]
    row = lambda i: (i, 0)
    fixed = lambda i: (layer, 0, 0)
    once = pl.Buffered(1)
    return pl.pallas_call(
        _out_ln_kernel,
        grid=(m // OUT_TM,),
        in_specs=[
            pl.BlockSpec((OUT_TM, D_MODEL), row),
            pl.BlockSpec((None, OUT_TM, D_PLE), lambda i: (layer, i, 0)),
            pl.BlockSpec((OUT_TM, D_A), row),
            pl.BlockSpec((OUT_TM, D_B), row),
            pl.BlockSpec((OUT_TM, D_A), lambda i: (i, _GA_BLK)),
            pl.BlockSpec((OUT_TM, D_B), lambda i: (i, _GB_BLK)),
            pl.BlockSpec((None, D_MODEL, D_MODEL), fixed, pipeline_mode=once),
            pl.BlockSpec((None, D_MODEL, D_MODEL), fixed, pipeline_mode=once),
            pl.BlockSpec((None, 1, D_MODEL), fixed),
            pl.BlockSpec((None, D_PLE, D_MODEL), fixed, pipeline_mode=once),
            pl.BlockSpec((None, 1, D_MODEL), fixed),
            pl.BlockSpec((None, 1, D_MODEL), fixed),
        ],
        out_specs=[pl.BlockSpec((OUT_TM, D_MODEL), row), pl.BlockSpec((OUT_TM, D_MODEL), row)],
        out_shape=[jax.ShapeDtypeStruct((m, D_MODEL), F32), jax.ShapeDtypeStruct((m, D_MODEL), BF16)],
        compiler_params=pltpu.CompilerParams(
            dimension_semantics=("parallel",), vmem_limit_bytes=_VMEM_LIMIT),
        name="out_ln",
    )(x2d, p3, ya2d, yb2d, pj1, pj1, wo, wg, bg3, wp, lg3, lb3)


def _rope_col_scale():
    sc = np.ones((1, D_P2), np.float32)
    sc[:, :D_B] = Q_SCALE
    qi0 = D_B + 2 * HEAD_DIM
    sc[:, qi0:qi0 + H_IDX * D_IDX] = QI_SCALE
    wi0 = qi0 + H_IDX * D_IDX + D_IDX
    sc[:, wi0:wi0 + H_IDX] = WI_SCALE
    return jnp.asarray(sc)


def _rope_tables(positions):
    def cos_sin(dim):
        inv = ROPE_THETA ** (-jnp.arange(0, dim, 2, dtype=F32) / dim)
        ang = positions.astype(F32)[..., None] * inv
        ang = jnp.concatenate([ang, ang], -1).reshape(-1, dim)
        return jnp.cos(ang), jnp.sin(ang)

    cos_h, sin_h = cos_sin(HEAD_DIM)
    sin_h = jnp.where(jnp.arange(LANES) < HEAD_DIM // 2, -sin_h, sin_h)
    cos_i, sin_i = cos_sin(D_IDX)
    return jnp.concatenate([cos_h, sin_h, cos_i, cos_i, sin_i, sin_i], axis=-1)


def _band_bias_diagonals(rel_bias):
    lead = rel_bias.shape[:-1]
    n_far = 3 * A_TQ - 1 - REL_CLIP
    n_near = A_DIAG - n_far - (2 * REL_CLIP + 1)
    return jnp.concatenate([
        jnp.broadcast_to(rel_bias[..., -1:], lead + (n_far,)),
        rel_bias[..., ::-1],
        jnp.broadcast_to(rel_bias[..., :1], lead + (n_near,))], axis=-1) * LOG2E


def kernel(x, p, positions, w_in, b_in, rel_bias, w_out, w_ple, w_ple_gate, b_ple_gate, ln_g, ln_b):
    b, s, d = x.shape
    m = b * s
    assert d == D_MODEL and s % B_KSTEP == 0 and m % P1_TM == 0
    topk = min(TOPK_MAX, s // 4)
    tabs = _rope_tables(positions)
    col_scale = _rope_col_scale()
    diag = _band_bias_diagonals(rel_bias)
    w_in_t = jnp.swapaxes(w_in, 1, 2)
    b_in3 = b_in[:, None, :]
    wo, wg, wp = w_out.astype(BF16), w_ple_gate.astype(BF16), w_ple.astype(BF16)
    bg3, lg3, lb3 = b_ple_gate[:, None, :], ln_g[:, None, :], ln_b[:, None, :]
    p3 = p.reshape(DEPTH, m, D_PLE)
    x2d = x.reshape(m, d)
    xb = x2d.astype(BF16)
    for i in range(DEPTH):
        pj1 = _proj_plain(xb, w_in_t, b_in3, i)
        pj2 = _proj_rope(xb, w_in_t, b_in3, col_scale, tabs, i)
        ya = _attn_a(pj1.reshape(b, s, D_P1), diag, i)
        yb = _attn_b(pj2.reshape(b, s, D_P2), topk)
        x2d, xb = _out_ln(x2d, p3, ya.reshape(m, D_A), yb.reshape(m, D_B), pj1,
                          wo, wg, bg3, wp, lg3, lb3, i)
    return x2d.reshape(b, s, d)
```

```python
import functools

import numpy as np
import jax
import jax.numpy as jnp
from jax import lax
from jax.experimental import pallas as pl
from jax.experimental.pallas import tpu as pltpu

D_MODEL = 2048
DEPTH = 4
CHUNK = 64
LEFT_CHUNKS = 8
HEAD_DIM = 128
D_A = 1024
D_B = 1024
H_A = 8
H_B = 8
REL_CLIP = 128
H_IDX = 8
D_IDX = 64
TOPK_MAX = 256
D_PLE = 256
ROPE_THETA = 10000.0
LN_EPS = 1e-5
NEG = -1e30
ALPHA = (2.0 * DEPTH) ** 0.25
LOG2E = 1.4426950408889634

F32 = jnp.float32
BF16 = jnp.bfloat16
LANES = 128

D_IN = 6984
W_BLK = 256
Q_SCALE = HEAD_DIM ** -0.5 * LOG2E
QI_SCALE = D_IDX ** -0.5
WI_SCALE = H_IDX ** -0.5

D_P1 = 5120
_QA_BLK, _KA_BLK, _VA_BLK, _GA_BLK, _GB_BLK = 0, 1, 2, 3, 4
_P1_SPLIT = 4096 // W_BLK
_P1_SKIP = (5376 - 4096) // W_BLK
_P2_SRC = (16, 17, 18, 19, 20, 25, 26, 27)
D_P2 = 7 * W_BLK + LANES
_P2_TAIL_VALID = D_IN - 27 * W_BLK
_QB_BLK = 0
_KB_BLK, _VB_BLK, _KI_BLK = 8, 9, 14
_QI_BLK = 2
_WI_OFF = H_IDX * D_IDX + D_IDX
P_SUB = 640
_P2_KINDS = ('hhhhh', 'hhhhp', 'iiiil')

P1_TM, P1_TN = 2048, 2 * W_BLK
P2_TM = 512
A_TQ = 256
A_DIAG = 4 * A_TQ
A_HG = 4
B_TQ = 256
B_KSTEP = 512
_SEL_ROUNDS, _SEL_PASSES = 8, 4
OUT_TM = 512

_VMEM_LIMIT = 56 * 1024 * 1024


def _nt_dot(a, b):
    return lax.dot_general(a, b, (((1,), (1,)), ((), ())), preferred_element_type=F32)


def _proj_plain_kernel(x_ref, wa_ref, wb_ref, ba_ref, bb_ref, o_ref):
    j = pl.program_id(1)
    w = jnp.concatenate([wa_ref[...].astype(BF16), wb_ref[...].astype(BF16)], axis=0)
    bias = jnp.concatenate([ba_ref[...], bb_ref[...]], axis=1)
    scale = jnp.where(j < D_A // P1_TN, Q_SCALE, 1.0)
    half = P1_TM // 2
    for r in range(2):
        rows = slice(r * half, (r + 1) * half)
        acc = _nt_dot(x_ref[rows, :], w) + bias
        o_ref[rows, :] = (acc * scale).astype(BF16)


def _proj_plain(xb, w_in_t, b_in3, layer):
    m = xb.shape[0]

    def src(u):
        return u + jnp.where(u >= _P1_SPLIT, _P1_SKIP, 0)

    def wspec(k):
        return pl.BlockSpec((None, W_BLK, D_MODEL), lambda i, j: (layer, src(2 * j + k), 0))

    def bspec(k):
        return pl.BlockSpec((None, 1, W_BLK), lambda i, j: (layer, 0, src(2 * j + k)))

    return pl.pallas_call(
        _proj_plain_kernel,
        grid=(m // P1_TM, D_P1 // P1_TN),
        in_specs=[pl.BlockSpec((P1_TM, D_MODEL), lambda i, j: (i, 0)),
                  wspec(0), wspec(1), bspec(0), bspec(1)],
        out_specs=pl.BlockSpec((P1_TM, P1_TN), lambda i, j: (i, j)),
        out_shape=jax.ShapeDtypeStruct((m, D_P1), BF16),
        compiler_params=pltpu.CompilerParams(
            dimension_semantics=("parallel", "arbitrary"), vmem_limit_bytes=_VMEM_LIMIT),
        name="proj_plain",
    )(xb, w_in_t, w_in_t, b_in3, b_in3)


def _proj_rope_kernel(x_ref, *refs):
    nsrc = len(_P2_SRC)
    w_refs, b_refs = refs[:nsrc], refs[nsrc:2 * nsrc]
    scale_ref, tab_ref, o_ref, wbf_ref, bias_ref = refs[2 * nsrc:]

    @pl.when(pl.program_id(0) == 0)
    def _():
        for n in range(nsrc):
            width = W_BLK if n < nsrc - 1 else LANES
            cols = slice(n * W_BLK, n * W_BLK + width)
            w = w_refs[n][:width, :]
            b = b_refs[n][:, :width]
            if n == nsrc - 1:
                w = jnp.where(lax.broadcasted_iota(jnp.int32, (width, 1), 0) < _P2_TAIL_VALID, w, 0.0)
                b = jnp.where(lax.broadcasted_iota(jnp.int32, (1, width), 1) < _P2_TAIL_VALID, b, 0.0)
            wbf_ref[cols, :] = w.astype(BF16)
            bias_ref[:, cols] = b * scale_ref[:, cols]

    x = x_ref[...]

    def tab(k):
        return tab_ref[:, k * LANES:(k + 1) * LANES]

    lane = lax.broadcasted_iota(jnp.int32, (1, LANES), 1)
    low_half = (lane & (D_IDX // 2)) == 0

    def rope_head(t):
        return t * tab(0) + pltpu.roll(t, 64, 1) * tab(1)

    def rope_idx(t):
        rot = jnp.where(low_half, -pltpu.roll(t, LANES - D_IDX // 2, 1), pltpu.roll(t, D_IDX // 2, 1))
        return t * tab(2) + rot * tab(3)

    for s, kinds in enumerate(_P2_KINDS):
        cols = slice(s * P_SUB, (s + 1) * P_SUB)
        acc = _nt_dot(x, wbf_ref[cols, :]) * scale_ref[:, cols] + bias_ref[:, cols]
        for k, kind in enumerate(kinds):
            t = acc[:, k * LANES:(k + 1) * LANES]
            if kind == 'h':
                t = rope_head(t)
            elif kind == 'i':
                t = rope_idx(t)
            elif kind == 'l':
                t = jnp.where(lane < D_IDX, rope_idx(t), t)
            c0 = s * P_SUB + k * LANES
            o_ref[:, c0:c0 + LANES] = t.astype(BF16)


def _proj_rope(xb, w_in_t, b_in3, col_scale, tabs, layer):
    m = xb.shape[0]
    once = pl.Buffered(1)
    wspecs = [pl.BlockSpec((None, W_BLK, D_MODEL), lambda i, u=u: (layer, u, 0), pipeline_mode=once)
              for u in _P2_SRC]
    bspecs = [pl.BlockSpec((None, 1, W_BLK), lambda i, u=u: (layer, 0, u)) for u in _P2_SRC]
    nsrc = len(_P2_SRC)
    return pl.pallas_call(
        _proj_rope_kernel,
        grid=(m // P2_TM,),
        in_specs=[pl.BlockSpec((P2_TM, D_MODEL), lambda i: (i, 0))] + wspecs + bspecs + [
            pl.BlockSpec((1, D_P2), lambda i: (0, 0)),
            pl.BlockSpec((P2_TM, 4 * LANES), lambda i: (i, 0)),
        ],
        out_specs=pl.BlockSpec((P2_TM, D_P2), lambda i: (i, 0)),
        out_shape=jax.ShapeDtypeStruct((m, D_P2), BF16),
        scratch_shapes=[pltpu.VMEM((D_P2, D_MODEL), BF16), pltpu.VMEM((1, D_P2), F32)],
        compiler_params=pltpu.CompilerParams(
            dimension_semantics=("arbitrary",), vmem_limit_bytes=_VMEM_LIMIT),
        name="proj_rope",
    )(xb, *([w_in_t] * nsrc), *([b_in3] * nsrc), col_scale, tabs)


def _attn_a_kernel(q_ref, k0_ref, k1_ref, k2_ref, v0_ref, v1_ref, v2_ref, diag_ref, o_ref, bias_ref):
    i = pl.program_id(1)
    nk = 3 * A_TQ

    @pl.when((pl.program_id(0) == 0) & (i == 0))
    def _():
        qc = lax.shift_right_logical(lax.broadcasted_iota(jnp.int32, (A_TQ, nk), 0), 6)
        kc = lax.shift_right_logical(lax.broadcasted_iota(jnp.int32, (A_TQ, nk), 1), 6)
        band = (kc >= qc) & (kc <= qc + LEFT_CHUNKS)
        for h in range(H_A):
            rows = jnp.broadcast_to(diag_ref[h:h + 1, :], (A_TQ, A_DIAG))
            t = pltpu.roll(rows, A_DIAG - (A_TQ - 1), 1, stride=1, stride_axis=0)
            bias_ref[h] = jnp.where(band, t[:, :nk], NEG)

    def head_cols(h):
        return slice(h * HEAD_DIM, (h + 1) * HEAD_DIM)

    def tile(mask_left):
        for h0 in range(0, H_A, A_HG):
            heads = range(h0, h0 + A_HG)
            logits = []
            for h in heads:
                cols = head_cols(h)
                k = jnp.concatenate([k0_ref[0, :, cols], k1_ref[0, :, cols], k2_ref[0, :, cols]], axis=0)
                logits.append(_nt_dot(q_ref[0, :, cols], k) + bias_ref[h])
            s = jnp.concatenate(logits, axis=0)
            if mask_left:
                kpos = (i - 2) * A_TQ + lax.broadcasted_iota(jnp.int32, s.shape, 1)
                s = jnp.where(kpos >= 0, s, NEG)
            m = jnp.max(s, axis=-1, keepdims=True)
            p = jnp.exp2(s - m)
            l = jnp.sum(p, axis=-1, keepdims=True)
            pb = p.astype(BF16)
            for j, h in enumerate(heads):
                cols = head_cols(h)
                rows = slice(j * A_TQ, (j + 1) * A_TQ)
                v = jnp.concatenate([v0_ref[0, :, cols], v1_ref[0, :, cols], v2_ref[0, :, cols]], axis=0)
                o = jnp.dot(pb[rows], v, preferred_element_type=F32) / l[rows]
                o_ref[0, :, cols] = o.astype(BF16)

    @pl.when(i < 2)
    def _():
        tile(True)

    @pl.when(i >= 2)
    def _():
        tile(False)


def _attn_a(pj1, diag, layer):
    b, s, _ = pj1.shape
    blk = (1, A_TQ, D_A)

    def kv_spec(col, back):
        return pl.BlockSpec(blk, lambda bb, i: (bb, jnp.maximum(i - back, 0), col))

    return pl.pallas_call(
        _attn_a_kernel,
        grid=(b, s // A_TQ),
        in_specs=[
            pl.BlockSpec(blk, lambda bb, i: (bb, i, _QA_BLK)),
            kv_spec(_KA_BLK, 2), kv_spec(_KA_BLK, 1), kv_spec(_KA_BLK, 0),
            kv_spec(_VA_BLK, 2), kv_spec(_VA_BLK, 1), kv_spec(_VA_BLK, 0),
            pl.BlockSpec((None, H_A, A_DIAG), lambda bb, i: (layer, 0, 0)),
        ],
        out_specs=pl.BlockSpec(blk, lambda bb, i: (bb, i, 0)),
        out_shape=jax.ShapeDtypeStruct((b, s, D_A), BF16),
        scratch_shapes=[pltpu.VMEM((H_A, A_TQ, 3 * A_TQ), F32)],
        compiler_params=pltpu.CompilerParams(
            dimension_semantics=("arbitrary", "arbitrary"), vmem_limit_bytes=_VMEM_LIMIT),
        name="attn_a",
    )(pj1, pj1, pj1, pj1, pj1, pj1, pj1, diag)


def _float_key(v):
    bits = int(np.float32(v).view(np.int32))
    return bits if bits >= 0 else bits ^ 0x7FFFFFFF


_KEY_LO = _float_key(NEG)
_KEY_HI = 0x7F800001


def _key_to_float(k):
    bits = jnp.where(k >= 0, k, k ^ 0x7FFFFFFF)
    return lax.bitcast_convert_type(bits, F32)


def _topk_mask_bias(score_ref, bias_ref, nk, topk, side_work):
    tq = score_ref.shape[0]
    nt = nk // LANES
    shape = (tq, LANES)
    row_groups = [slice(g * LANES, (g + 1) * LANES) for g in range(tq // LANES)]

    def tile(rows, t):
        return score_ref[rows, t * LANES:(t + 1) * LANES]

    def row_reduce(tile_fn, combine, lane_reduce):
        out = []
        for rows in row_groups:
            acc = tile_fn(tile(rows, 0), 0, rows)
            for t in range(1, nt):
                acc = combine(acc, tile_fn(tile(rows, t), t, rows))
            out.append(jnp.broadcast_to(lane_reduce(acc, axis=1, keepdims=True), (LANES, LANES)))
        return jnp.concatenate(out, axis=0)

    def count(pred):
        return row_reduce(lambda sc, t, rows: jnp.where(pred(sc, t, rows), 1.0, 0.0),
                          jnp.add, jnp.sum)

    def step(lo, hi, c_lo, mid):
        th = _key_to_float(mid)
        c = count(lambda sc, t, rows: sc >= th[rows])
        ge = c >= topk
        return jnp.where(ge, mid, lo), jnp.where(ge, hi, mid), jnp.where(ge, c, c_lo)

    def bisect_round(r, carry):
        lo, hi, c_lo = carry
        for _ in range(_SEL_PASSES):
            lo, hi, c_lo = step(lo, hi, c_lo, lo + lax.shift_right_logical(hi - lo, 1))
        side_work(r)
        return lo, hi, c_lo

    lo, _, c_lo = lax.fori_loop(
        0, _SEL_ROUNDS, bisect_round,
        (jnp.full(shape, _KEY_LO, jnp.int32), jnp.full(shape, _KEY_HI, jnp.int32),
         jnp.full(shape, float(nk), F32)))
    thr = _key_to_float(lo)
    partial = jnp.where((c_lo > topk) & (thr > 0.5 * NEG), 1.0, 0.0)
    any_partial = jnp.max(jnp.max(partial, axis=1, keepdims=True), axis=0, keepdims=True)[0, 0] > 0.0

    @pl.when(jnp.logical_not(any_partial))
    def _():
        for rows in row_groups:
            for t in range(nt):
                sc = tile(rows, t)
                sel = (sc >= thr[rows]) & (sc > 0.5 * NEG)
                bias_ref[rows, t * LANES:(t + 1) * LANES] = jnp.where(sel, 0.0, NEG)

    @pl.when(any_partial)
    def _():
        need = topk - count(lambda sc, t, rows: sc > thr[rows])
        lane = lax.broadcasted_iota(jnp.int32, (LANES, LANES), 1)

        def tie_bisect(b, jsel):
            cand = jsel + jnp.left_shift(jnp.int32(1), 10 - b)
            c = count(lambda sc, t, rows: (sc == thr[rows]) & (lane + t * LANES < cand[rows]))
            return jnp.where(c < need, cand, jsel)

        jsel = lax.fori_loop(0, 11, tie_bisect, jnp.zeros(shape, jnp.int32))
        for rows in row_groups:
            for t in range(nt):
                sc = tile(rows, t)
                sel = (sc > thr[rows]) | ((sc == thr[rows]) & (lane + t * LANES <= jsel[rows]))
                sel = sel & (sc > 0.5 * NEG)
                bias_ref[rows, t * LANES:(t + 1) * LANES] = jnp.where(sel, 0.0, NEG)


def _attn_b_block(nk, i, qb_ref, kb_ref, vb_ref, qiw_ref, kiw_ref, o_ref, score_ref, bias_ref,
                  s_ref, qh_ref, topk):
    tq = score_ref.shape[0]
    ki = kiw_ref[0, :nk, :D_IDX]
    vb = vb_ref[0, :nk, :]
    wi = qiw_ref[0, :, _WI_OFF:_WI_OFF + H_IDX].astype(F32)

    score = jnp.zeros((tq, nk), F32)
    for h in range(H_IDX):
        qh = qiw_ref[0, :, h * D_IDX:(h + 1) * D_IDX]
        score = score + wi[:, h:h + 1] * jnp.maximum(_nt_dot(qh, ki), 0.0)
    t_pos = i * tq + lax.broadcasted_iota(jnp.int32, (tq, nk), 0)
    visible_end = (t_pos // CHUNK + 1) * CHUNK
    key_pos = lax.broadcasted_iota(jnp.int32, (tq, nk), 1)
    score_ref[:, :nk] = jnp.where(key_pos < visible_end, score, NEG)
    for h in range(H_B):
        qh_ref[h] = qb_ref[0, :, h * HEAD_DIM:(h + 1) * HEAD_DIM]

    def qk_logits(r):
        for j in range(H_B // _SEL_ROUNDS):
            h = r * (H_B // _SEL_ROUNDS) + j
            s_ref[h, :, :nk] = _nt_dot(qh_ref[h], kb_ref[0, :nk, :])

    _topk_mask_bias(score_ref, bias_ref, nk, topk, qk_logits)

    for h in range(H_B):
        sc = s_ref[h, :, :nk] + bias_ref[:, :nk]
        m = jnp.max(sc, axis=-1, keepdims=True)
        p = jnp.exp2(sc - m)
        l = jnp.sum(p, axis=-1, keepdims=True)
        o = jnp.dot(p.astype(BF16), vb, preferred_element_type=F32) / l
        o_ref[0, :, h * HEAD_DIM:(h + 1) * HEAD_DIM] = o.astype(BF16)


def _attn_b_kernel(qb_ref, kb_ref, vb_ref, qiw_ref, kiw_ref, o_ref, score_ref, bias_ref, s_ref,
                   qh_ref, *, topk):
    i = pl.program_id(1)
    tq, s = score_ref.shape
    per = B_KSTEP // tq
    for n in range(1, s // B_KSTEP + 1):
        @pl.when(i // per == n - 1)
        def _(n=n):
            _attn_b_block(n * B_KSTEP, i, qb_ref, kb_ref, vb_ref, qiw_ref, kiw_ref, o_ref,
                          score_ref, bias_ref, s_ref, qh_ref, topk)


def _attn_b(pj2, topk):
    b, s, _ = pj2.shape
    return pl.pallas_call(
        functools.partial(_attn_b_kernel, topk=topk),
        grid=(b, s // B_TQ),
        in_specs=[
            pl.BlockSpec((1, B_TQ, D_B), lambda bb, i: (bb, i, _QB_BLK)),
            pl.BlockSpec((1, s, HEAD_DIM), lambda bb, i: (bb, 0, _KB_BLK)),
            pl.BlockSpec((1, s, HEAD_DIM), lambda bb, i: (bb, 0, _VB_BLK)),
            pl.BlockSpec((1, B_TQ, P_SUB), lambda bb, i: (bb, i, _QI_BLK)),
            pl.BlockSpec((1, s, LANES), lambda bb, i: (bb, 0, _KI_BLK)),
        ],
        out_specs=pl.BlockSpec((1, B_TQ, D_B), lambda bb, i: (bb, i, 0)),
        out_shape=jax.ShapeDtypeStruct((b, s, D_B), BF16),
        scratch_shapes=[pltpu.VMEM((B_TQ, s), F32), pltpu.VMEM((B_TQ, s), F32),
                        pltpu.VMEM((H_B, B_TQ, s), F32), pltpu.VMEM((H_B, B_TQ, HEAD_DIM), BF16)],
        compiler_params=pltpu.CompilerParams(
            dimension_semantics=("parallel", "parallel"), vmem_limit_bytes=_VMEM_LIMIT),
        name="attn_b",
    )(pj2, pj2, pj2, pj2, pj2)


def _sigmoid(v):
    return 1.0 / (1.0 + jnp.exp(-v))


def _out_ln_kernel(x_ref, p_ref, ya_ref, yb_ref, ga_ref, gb_ref, wo_ref, wg_ref, bg_ref, wp_ref,
                   lg_ref, lb_ref, o_ref, ob_ref):
    x = x_ref[...]
    ga = ga_ref[...].astype(F32)
    gb = gb_ref[...].astype(F32)
    ua = (ya_ref[...].astype(F32) * (ga * _sigmoid(ga))).astype(BF16)
    ub = (yb_ref[...].astype(F32) * (gb * _sigmoid(gb))).astype(BF16)
    y = (jnp.dot(ua, wo_ref[:D_A, :], preferred_element_type=F32)
         + jnp.dot(ub, wo_ref[D_A:, :], preferred_element_type=F32))
    gate = _sigmoid(jnp.dot(x.astype(BF16), wg_ref[...], preferred_element_type=F32) + bg_ref[...])
    ple = gate * jnp.dot(p_ref[...].astype(BF16), wp_ref[...], preferred_element_type=F32)
    z = ALPHA * x + y + ple
    mu = jnp.mean(z, axis=-1, keepdims=True)
    zc = z - mu
    var = jnp.mean(zc * zc, axis=-1, keepdims=True)
    out = zc * lax.rsqrt(var + LN_EPS) * lg_ref[...] + lb_ref[...]
    o_ref[...] = out
    ob_ref[...] = out.astype(BF16)


def _out_ln(x2d, p3, ya2d, yb2d, pj1, wo, wg, bg3, wp, lg3, lb3, layer):
    m = x2d.shape[0]
    row = lambda i: (i, 0)
    fixed = lambda i: (layer, 0, 0)
    once = pl.Buffered(1)
    return pl.pallas_call(
        _out_ln_kernel,
        grid=(m // OUT_TM,),
        in_specs=[
            pl.BlockSpec((OUT_TM, D_MODEL), row),
            pl.BlockSpec((None, OUT_TM, D_PLE), lambda i: (layer, i, 0)),
            pl.BlockSpec((OUT_TM, D_A), row),
            pl.BlockSpec((OUT_TM, D_B), row),
            pl.BlockSpec((OUT_TM, D_A), lambda i: (i, _GA_BLK)),
            pl.BlockSpec((OUT_TM, D_B), lambda i: (i, _GB_BLK)),
            pl.BlockSpec((None, D_MODEL, D_MODEL), fixed, pipeline_mode=once),
            pl.BlockSpec((None, D_MODEL, D_MODEL), fixed, pipeline_mode=once),
            pl.BlockSpec((None, 1, D_MODEL), fixed),
            pl.BlockSpec((None, D_PLE, D_MODEL), fixed, pipeline_mode=once),
            pl.BlockSpec((None, 1, D_MODEL), fixed),
            pl.BlockSpec((None, 1, D_MODEL), fixed),
        ],
        out_specs=[pl.BlockSpec((OUT_TM, D_MODEL), row), pl.BlockSpec((OUT_TM, D_MODEL), row)],
        out_shape=[jax.ShapeDtypeStruct((m, D_MODEL), F32), jax.ShapeDtypeStruct((m, D_MODEL), BF16)],
        compiler_params=pltpu.CompilerParams(
            dimension_semantics=("parallel",), vmem_limit_bytes=_VMEM_LIMIT),
        name="out_ln",
    )(x2d, p3, ya2d, yb2d, pj1, pj1, wo, wg, bg3, wp, lg3, lb3)


def _rope_col_scale():
    sc = np.ones((1, D_P2), np.float32)
    sc[:, :D_B] = Q_SCALE
    qi0 = D_B + 2 * HEAD_DIM
    sc[:, qi0:qi0 + H_IDX * D_IDX] = QI_SCALE
    wi0 = qi0 + H_IDX * D_IDX + D_IDX
    sc[:, wi0:wi0 + H_IDX] = WI_SCALE
    return jnp.asarray(sc)


def _rope_tables(positions):
    def cos_sin(dim):
        inv = ROPE_THETA ** (-jnp.arange(0, dim, 2, dtype=F32) / dim)
        ang = positions.astype(F32)[..., None] * inv
        ang = jnp.concatenate([ang, ang], -1).reshape(-1, dim)
        return jnp.cos(ang), jnp.sin(ang)

    cos_h, sin_h = cos_sin(HEAD_DIM)
    sin_h = jnp.where(jnp.arange(LANES) < HEAD_DIM // 2, -sin_h, sin_h)
    cos_i, sin_i = cos_sin(D_IDX)
    return jnp.concatenate([cos_h, sin_h, cos_i, cos_i, sin_i, sin_i], axis=-1)


def _band_bias_diagonals(rel_bias):
    lead = rel_bias.shape[:-1]
    n_far = 3 * A_TQ - 1 - REL_CLIP
    n_near = A_DIAG - n_far - (2 * REL_CLIP + 1)
    return jnp.concatenate([
        jnp.broadcast_to(rel_bias[..., -1:], lead + (n_far,)),
        rel_bias[..., ::-1],
        jnp.broadcast_to(rel_bias[..., :1], lead + (n_near,))], axis=-1) * LOG2E


def kernel(x, p, positions, w_in, b_in, rel_bias, w_out, w_ple, w_ple_gate, b_ple_gate, ln_g, ln_b):
    b, s, d = x.shape
    m = b * s
    assert d == D_MODEL and s % B_KSTEP == 0 and m % P1_TM == 0
    topk = min(TOPK_MAX, s // 4)
    tabs = _rope_tables(positions)
    col_scale = _rope_col_scale()
    diag = _band_bias_diagonals(rel_bias)
    w_in_t = jnp.swapaxes(w_in, 1, 2)
    b_in3 = b_in[:, None, :]
    wo, wg, wp = w_out.astype(BF16), w_ple_gate.astype(BF16), w_ple.astype(BF16)
    bg3, lg3, lb3 = b_ple_gate[:, None, :], ln_g[:, None, :], ln_b[:, None, :]
    p3 = p.reshape(DEPTH, m, D_PLE)
    x2d = x.reshape(m, d)
    xb = x2d.astype(BF16)
    for i in range(DEPTH):
        pj1 = _proj_plain(xb, w_in_t, b_in3, i)
        pj2 = _proj_rope(xb, w_in_t, b_in3, col_scale, tabs, i)
        ya = _attn_a(pj1.reshape(b, s, D_P1), diag, i)
        yb = _attn_b(pj2.reshape(b, s, D_P2), topk)
        x2d, xb = _out_ln(x2d, p3, ya.reshape(m, D_A), yb.reshape(m, D_B), pj1,
                          wo, wg, bg3, wp, lg3, lb3, i)
    return x2d.reshape(b, s, d)
```

```python
import functools

import numpy as np
import jax
import jax.numpy as jnp
from jax import lax
from jax.experimental import pallas as pl
from jax.experimental.pallas import tpu as pltpu

D_MODEL = 2048
DEPTH = 4
CHUNK = 64
LEFT_CHUNKS = 8
HEAD_DIM = 128
D_A = 1024
D_B = 1024
H_A = 8
H_B = 8
REL_CLIP = 128
H_IDX = 8
D_IDX = 64
TOPK_MAX = 256
D_PLE = 256
ROPE_THETA = 10000.0
LN_EPS = 1e-5
NEG = -1e30
ALPHA = (2.0 * DEPTH) ** 0.25
LOG2E = 1.4426950408889634

F32 = jnp.float32
BF16 = jnp.bfloat16
LANES = 128

D_IN = 6984
W_BLK = 256
Q_SCALE = HEAD_DIM ** -0.5 * LOG2E
QI_SCALE = D_IDX ** -0.5
WI_SCALE = H_IDX ** -0.5

D_P1 = 5120
_QA_BLK, _KA_BLK, _VA_BLK, _GA_BLK, _GB_BLK = 0, 1, 2, 3, 4
_P1_SPLIT = 4096 // W_BLK
_P1_SKIP = (5376 - 4096) // W_BLK
_P2_SRC = (16, 17, 18, 19, 20, 25, 26, 27)
D_P2 = 7 * W_BLK + LANES
_P2_TAIL_VALID = D_IN - 27 * W_BLK
_QB_BLK = 0
_KB_BLK, _VB_BLK, _KI_BLK = 8, 9, 14
_QI_BLK = 2
_WI_OFF = H_IDX * D_IDX + D_IDX
P_SUB = 640
_P2_KINDS = ('hhhhh', 'hhhhp', 'iiiil')

P1_TM, P1_TN = 2048, 2 * W_BLK
P2_TM = 512
A_TQ = 256
A_DIAG = 4 * A_TQ
A_HG = 4
B_TQ = 256
B_KSTEP = 512
_SEL_ROUNDS, _SEL_PASSES = 8, 4
OUT_TM = 512

_VMEM_LIMIT = 56 * 1024 * 1024


def _nt_dot(a, b):
    return lax.dot_general(a, b, (((1,), (1,)), ((), ())), preferred_element_type=F32)


def _proj_plain_kernel(x_ref, wa_ref, wb_ref, ba_ref, bb_ref, o_ref):
    j = pl.program_id(1)
    w = jnp.concatenate([wa_ref[...].astype(BF16), wb_ref[...].astype(BF16)], axis=0)
    bias = jnp.concatenate([ba_ref[...], bb_ref[...]], axis=1)
    scale = jnp.where(j < D_A // P1_TN, Q_SCALE, 1.0)
    half = P1_TM // 2
    for r in range(2):
        rows = slice(r * half, (r + 1) * half)
        acc = _nt_dot(x_ref[rows, :], w) + bias
        o_ref[rows, :] = (acc * scale).astype(BF16)


def _proj_plain(xb, w_in_t, b_in3, layer):
    m = xb.shape[0]

    def src(u):
        return u + jnp.where(u >= _P1_SPLIT, _P1_SKIP, 0)

    def wspec(k):
        return pl.BlockSpec((None, W_BLK, D_MODEL), lambda i, j: (layer, src(2 * j + k), 0))

    def bspec(k):
        return pl.BlockSpec((None, 1, W_BLK), lambda i, j: (layer, 0, src(2 * j + k)))

    return pl.pallas_call(
        _proj_plain_kernel,
        grid=(m // P1_TM, D_P1 // P1_TN),
        in_specs=[pl.BlockSpec((P1_TM, D_MODEL), lambda i, j: (i, 0)),
                  wspec(0), wspec(1), bspec(0), bspec(1)],
        out_specs=pl.BlockSpec((P1_TM, P1_TN), lambda i, j: (i, j)),
        out_shape=jax.ShapeDtypeStruct((m, D_P1), BF16),
        compiler_params=pltpu.CompilerParams(
            dimension_semantics=("parallel", "arbitrary"), vmem_limit_bytes=_VMEM_LIMIT),
        name="proj_plain",
    )(xb, w_in_t, w_in_t, b_in3, b_in3)


def _proj_rope_kernel(x_ref, *refs):
    nsrc = len(_P2_SRC)
    w_refs, b_refs = refs[:nsrc], refs[nsrc:2 * nsrc]
    scale_ref, tab_ref, o_ref, wbf_ref, bias_ref = refs[2 * nsrc:]

    @pl.when(pl.program_id(0) == 0)
    def _():
        for n in range(nsrc):
            width = W_BLK if n < nsrc - 1 else LANES
            cols = slice(n * W_BLK, n * W_BLK + width)
            w = w_refs[n][:width, :]
            b = b_refs[n][:, :width]
            if n == nsrc - 1:
                w = jnp.where(lax.broadcasted_iota(jnp.int32, (width, 1), 0) < _P2_TAIL_VALID, w, 0.0)
                b = jnp.where(lax.broadcasted_iota(jnp.int32, (1, width), 1) < _P2_TAIL_VALID, b, 0.0)
            wbf_ref[cols, :] = w.astype(BF16)
            bias_ref[:, cols] = b * scale_ref[:, cols]

    x = x_ref[...]

    def tab(k):
        return tab_ref[:, k * LANES:(k + 1) * LANES]

    lane = lax.broadcasted_iota(jnp.int32, (1, LANES), 1)
    low_half = (lane & (D_IDX // 2)) == 0

    def rope_head(t):
        return t * tab(0) + pltpu.roll(t, 64, 1) * tab(1)

    def rope_idx(t):
        rot = jnp.where(low_half, -pltpu.roll(t, LANES - D_IDX // 2, 1), pltpu.roll(t, D_IDX // 2, 1))
        return t * tab(2) + rot * tab(3)

    for s, kinds in enumerate(_P2_KINDS):
        cols = slice(s * P_SUB, (s + 1) * P_SUB)
        acc = _nt_dot(x, wbf_ref[cols, :]) * scale_ref[:, cols] + bias_ref[:, cols]
        for k, kind in enumerate(kinds):
            t = acc[:, k * LANES:(k + 1) * LANES]
            if kind == 'h':
                t = rope_head(t)
            elif kind == 'i':
                t = rope_idx(t)
            elif kind == 'l':
                t = jnp.where(lane < D_IDX, rope_idx(t), t)
            c0 = s * P_SUB + k * LANES
            o_ref[:, c0:c0 + LANES] = t.astype(BF16)


def _proj_rope(xb, w_in_t, b_in3, col_scale, tabs, layer):
    m = xb.shape[0]
    once = pl.Buffered(1)
    wspecs = [pl.BlockSpec((None, W_BLK, D_MODEL), lambda i, u=u: (layer, u, 0), pipeline_mode=once)
              for u in _P2_SRC]
    bspecs = [pl.BlockSpec((None, 1, W_BLK), lambda i, u=u: (layer, 0, u)) for u in _P2_SRC]
    nsrc = len(_P2_SRC)
    return pl.pallas_call(
        _proj_rope_kernel,
        grid=(m // P2_TM,),
        in_specs=[pl.BlockSpec((P2_TM, D_MODEL), lambda i: (i, 0))] + wspecs + bspecs + [
            pl.BlockSpec((1, D_P2), lambda i: (0, 0)),
            pl.BlockSpec((P2_TM, 4 * LANES), lambda i: (i, 0)),
        ],
        out_specs=pl.BlockSpec((P2_TM, D_P2), lambda i: (i, 0)),
        out_shape=jax.ShapeDtypeStruct((m, D_P2), BF16),
        scratch_shapes=[pltpu.VMEM((D_P2, D_MODEL), BF16), pltpu.VMEM((1, D_P2), F32)],
        compiler_params=pltpu.CompilerParams(
            dimension_semantics=("arbitrary",), vmem_limit_bytes=_VMEM_LIMIT),
        name="proj_rope",
    )(xb, *([w_in_t] * nsrc), *([b_in3] * nsrc), col_scale, tabs)


def _attn_a_kernel(q_ref, k0_ref, k1_ref, k2_ref, v0_ref, v1_ref, v2_ref, diag_ref, o_ref, bias_ref):
    i = pl.program_id(1)
    nk = 3 * A_TQ

    @pl.when((pl.program_id(0) == 0) & (i == 0))
    def _():
        qc = lax.shift_right_logical(lax.broadcasted_iota(jnp.int32, (A_TQ, nk), 0), 6)
        kc = lax.shift_right_logical(lax.broadcasted_iota(jnp.int32, (A_TQ, nk), 1), 6)
        band = (kc >= qc) & (kc <= qc + LEFT_CHUNKS)
        for h in range(H_A):
            rows = jnp.broadcast_to(diag_ref[h:h + 1, :], (A_TQ, A_DIAG))
            t = pltpu.roll(rows, A_DIAG - (A_TQ - 1), 1, stride=1, stride_axis=0)
            bias_ref[h] = jnp.where(band, t[:, :nk], NEG)

    def head_cols(h):
        return slice(h * HEAD_DIM, (h + 1) * HEAD_DIM)

    def tile(mask_left):
        for h0 in range(0, H_A, A_HG):
            heads = range(h0, h0 + A_HG)
            logits = []
            for h in heads:
                cols = head_cols(h)
                k = jnp.concatenate([k0_ref[0, :, cols], k1_ref[0, :, cols], k2_ref[0, :, cols]], axis=0)
                logits.append(_nt_dot(q_ref[0, :, cols], k) + bias_ref[h])
            s = jnp.concatenate(logits, axis=0)
            if mask_left:
                kpos = (i - 2) * A_TQ + lax.broadcasted_iota(jnp.int32, s.shape, 1)
                s = jnp.where(kpos >= 0, s, NEG)
            m = jnp.max(s, axis=-1, keepdims=True)
            p = jnp.exp2(s - m)
            l = jnp.sum(p, axis=-1, keepdims=True)
            pb = p.astype(BF16)
            for j, h in enumerate(heads):
                cols = head_cols(h)
                rows = slice(j * A_TQ, (j + 1) * A_TQ)
                v = jnp.concatenate([v0_ref[0, :, cols], v1_ref[0, :, cols], v2_ref[0, :, cols]], axis=0)
                o = jnp.dot(pb[rows], v, preferred_element_type=F32) / l[rows]
                o_ref[0, :, cols] = o.astype(BF16)

    @pl.when(i < 2)
    def _():
        tile(True)

    @pl.when(i >= 2)
    def _():
        tile(False)


def _attn_a(pj1, diag, layer):
    b, s, _ = pj1.shape
    blk = (1, A_TQ, D_A)

    def kv_spec(col, back):
        return pl.BlockSpec(blk, lambda bb, i: (bb, jnp.maximum(i - back, 0), col))

    return pl.pallas_call(
        _attn_a_kernel,
        grid=(b, s // A_TQ),
        in_specs=[
            pl.BlockSpec(blk, lambda bb, i: (bb, i, _QA_BLK)),
            kv_spec(_KA_BLK, 2), kv_spec(_KA_BLK, 1), kv_spec(_KA_BLK, 0),
            kv_spec(_VA_BLK, 2), kv_spec(_VA_BLK, 1), kv_spec(_VA_BLK, 0),
            pl.BlockSpec((None, H_A, A_DIAG), lambda bb, i: (layer, 0, 0)),
        ],
        out_specs=pl.BlockSpec(blk, lambda bb, i: (bb, i, 0)),
        out_shape=jax.ShapeDtypeStruct((b, s, D_A), BF16),
        scratch_shapes=[pltpu.VMEM((H_A, A_TQ, 3 * A_TQ), F32)],
        compiler_params=pltpu.CompilerParams(
            dimension_semantics=("arbitrary", "arbitrary"), vmem_limit_bytes=_VMEM_LIMIT),
        name="attn_a",
    )(pj1, pj1, pj1, pj1, pj1, pj1, pj1, diag)


def _float_key(v):
    bits = int(np.float32(v).view(np.int32))
    return bits if bits >= 0 else bits ^ 0x7FFFFFFF


_KEY_LO = _float_key(NEG)
_KEY_HI = 0x7F800001


def _key_to_float(k):
    bits = jnp.where(k >= 0, k, k ^ 0x7FFFFFFF)
    return lax.bitcast_convert_type(bits, F32)


def _topk_mask_bias(score_ref, bias_ref, nk, topk, side_work):
    tq = score_ref.shape[0]
    nt = nk // LANES
    shape = (tq, LANES)
    row_groups = [slice(g * LANES, (g + 1) * LANES) for g in range(tq // LANES)]

    def tile(rows, t):
        return score_ref[rows, t * LANES:(t + 1) * LANES]

    def row_reduce(tile_fn, combine, lane_reduce):
        out = []
        for rows in row_groups:
            acc = tile_fn(tile(rows, 0), 0, rows)
            for t in range(1, nt):
                acc = combine(acc, tile_fn(tile(rows, t), t, rows))
            out.append(jnp.broadcast_to(lane_reduce(acc, axis=1, keepdims=True), (LANES, LANES)))
        return jnp.concatenate(out, axis=0)

    def count(pred):
        return row_reduce(lambda sc, t, rows: jnp.where(pred(sc, t, rows), 1.0, 0.0),
                          jnp.add, jnp.sum)

    def step(lo, hi, c_lo, mid):
        th = _key_to_float(mid)
        c = count(lambda sc, t, rows: sc >= th[rows])
        ge = c >= topk
        return jnp.where(ge, mid, lo), jnp.where(ge, hi, mid), jnp.where(ge, c, c_lo)

    def bisect_round(r, carry):
        lo, hi, c_lo = carry
        for _ in range(_SEL_PASSES):
            lo, hi, c_lo = step(lo, hi, c_lo, lo + lax.shift_right_logical(hi - lo, 1))
        side_work(r)
        return lo, hi, c_lo

    lo, _, c_lo = lax.fori_loop(
        0, _SEL_ROUNDS, bisect_round,
        (jnp.full(shape, _KEY_LO, jnp.int32), jnp.full(shape, _KEY_HI, jnp.int32),
         jnp.full(shape, float(nk), F32)))
    thr = _key_to_float(lo)
    partial = jnp.where((c_lo > topk) & (thr > 0.5 * NEG), 1.0, 0.0)
    any_partial = jnp.max(jnp.max(partial, axis=1, keepdims=True), axis=0, keepdims=True)[0, 0] > 0.0

    @pl.when(jnp.logical_not(any_partial))
    def _():
        for rows in row_groups:
            for t in range(nt):
                sc = tile(rows, t)
                sel = (sc >= thr[rows]) & (sc > 0.5 * NEG)
                bias_ref[rows, t * LANES:(t + 1) * LANES] = jnp.where(sel, 0.0, NEG)

    @pl.when(any_partial)
    def _():
        need_all = topk - count(lambda sc, t, rows: sc > thr[rows])
        lane = lax.broadcasted_iota(jnp.int32, (LANES, LANES), 1)

        def lane_count(v):
            return jnp.broadcast_to(jnp.sum(v, axis=1, keepdims=True), (LANES, LANES))

        for rows in row_groups:
            thr_g, need = thr[rows], need_all[rows]
            cum = jnp.zeros((LANES, LANES), F32)
            before = cum
            boundary = cum
            t_star = jnp.zeros((LANES, LANES), jnp.int32)
            for t in range(nt):
                tie = jnp.where(tile(rows, t) == thr_g, 1.0, 0.0)
                new_cum = cum + lane_count(tie)
                hit = (cum < need) & (new_cum >= need)
                t_star = jnp.where(hit, t, t_star)
                before = jnp.where(hit, cum, before)
                boundary = jnp.where(hit, tie, boundary)
                cum = new_cum
            rem = need - before
            jsel = jnp.zeros((LANES, LANES), jnp.int32)
            for b in range(6, -1, -1):
                cand = jsel + (1 << b)
                c = lane_count(jnp.where(lane < cand, boundary, 0.0))
                jsel = jnp.where(c < rem, cand, jsel)
            for t in range(nt):
                sc = tile(rows, t)
                tie_sel = (t_star > t) | ((t_star == t) & (lane <= jsel))
                sel = (sc > thr_g) | ((sc == thr_g) & tie_sel)
                sel = sel & (sc > 0.5 * NEG)
                bias_ref[rows, t * LANES:(t + 1) * LANES] = jnp.where(sel, 0.0, NEG)


def _attn_b_block(nk, i, qb_ref, kb_ref, vb_ref, qiw_ref, kiw_ref, o_ref, score_ref, bias_ref,
                  s_ref, qh_ref, topk):
    tq = score_ref.shape[0]
    ki = kiw_ref[0, :nk, :D_IDX]
    vb = vb_ref[0, :nk, :]
    wi = qiw_ref[0, :, _WI_OFF:_WI_OFF + H_IDX].astype(F32)

    score = jnp.zeros((tq, nk), F32)
    for h in range(H_IDX):
        qh = qiw_ref[0, :, h * D_IDX:(h + 1) * D_IDX]
        score = score + wi[:, h:h + 1] * jnp.maximum(_nt_dot(qh, ki), 0.0)
    t_pos = i * tq + lax.broadcasted_iota(jnp.int32, (tq, nk), 0)
    visible_end = (t_pos // CHUNK + 1) * CHUNK
    key_pos = lax.broadcasted_iota(jnp.int32, (tq, nk), 1)
    score_ref[:, :nk] = jnp.where(key_pos < visible_end, score, NEG)
    for h in range(H_B):
        qh_ref[h] = qb_ref[0, :, h * HEAD_DIM:(h + 1) * HEAD_DIM]

    def qk_logits(r):
        for j in range(H_B // _SEL_ROUNDS):
            h = r * (H_B // _SEL_ROUNDS) + j
            s_ref[h, :, :nk] = _nt_dot(qh_ref[h], kb_ref[0, :nk, :])

    _topk_mask_bias(score_ref, bias_ref, nk, topk, qk_logits)

    for h in range(H_B):
        sc = s_ref[h, :, :nk] + bias_ref[:, :nk]
        m = jnp.max(sc, axis=-1, keepdims=True)
        p = jnp.exp2(sc - m)
        l = jnp.sum(p, axis=-1, keepdims=True)
        o = jnp.dot(p.astype(BF16), vb, preferred_element_type=F32) / l
        o_ref[0, :, h * HEAD_DIM:(h + 1) * HEAD_DIM] = o.astype(BF16)


def _attn_b_kernel(qb_ref, kb_ref, vb_ref, qiw_ref, kiw_ref, o_ref, score_ref, bias_ref, s_ref,
                   qh_ref, *, topk):
    i = pl.program_id(1)
    tq, s = score_ref.shape
    per = B_KSTEP // tq
    for n in range(1, s // B_KSTEP + 1):
        @pl.when(i // per == n - 1)
        def _(n=n):
            _attn_b_block(n * B_KSTEP, i, qb_ref, kb_ref, vb_ref, qiw_ref, kiw_ref, o_ref,
                          score_ref, bias_ref, s_ref, qh_ref, topk)


def _attn_b(pj2, topk):
    b, s, _ = pj2.shape
    return pl.pallas_call(
        functools.partial(_attn_b_kernel, topk=topk),
        grid=(b, s // B_TQ),
        in_specs=[
            pl.BlockSpec((1, B_TQ, D_B), lambda bb, i: (bb, i, _QB_BLK)),
            pl.BlockSpec((1, s, HEAD_DIM), lambda bb, i: (bb, 0, _KB_BLK)),
            pl.BlockSpec((1, s, HEAD_DIM), lambda bb, i: (bb, 0, _VB_BLK)),
            pl.BlockSpec((1, B_TQ, P_SUB), lambda bb, i: (bb, i, _QI_BLK)),
            pl.BlockSpec((1, s, LANES), lambda bb, i: (bb, 0, _KI_BLK)),
        ],
        out_specs=pl.BlockSpec((1, B_TQ, D_B), lambda bb, i: (bb, i, 0)),
        out_shape=jax.ShapeDtypeStruct((b, s, D_B), BF16),
        scratch_shapes=[pltpu.VMEM((B_TQ, s), F32), pltpu.VMEM((B_TQ, s), F32),
                        pltpu.VMEM((H_B, B_TQ, s), F32), pltpu.VMEM((H_B, B_TQ, HEAD_DIM), BF16)],
        compiler_params=pltpu.CompilerParams(
            dimension_semantics=("parallel", "parallel"), vmem_limit_bytes=_VMEM_LIMIT),
        name="attn_b",
    )(pj2, pj2, pj2, pj2, pj2)


def _sigmoid(v):
    return 1.0 / (1.0 + jnp.exp(-v))


def _out_ln_kernel(x_ref, p_ref, ya_ref, yb_ref, ga_ref, gb_ref, wo_ref, wg_ref, bg_ref, wp_ref,
                   lg_ref, lb_ref, o_ref, ob_ref):
    x = x_ref[...]
    ga = ga_ref[...].astype(F32)
    gb = gb_ref[...].astype(F32)
    ua = (ya_ref[...].astype(F32) * (ga * _sigmoid(ga))).astype(BF16)
    ub = (yb_ref[...].astype(F32) * (gb * _sigmoid(gb))).astype(BF16)
    y = (jnp.dot(ua, wo_ref[:D_A, :], preferred_element_type=F32)
         + jnp.dot(ub, wo_ref[D_A:, :], preferred_element_type=F32))
    gate = _sigmoid(jnp.dot(x.astype(BF16), wg_ref[...], preferred_element_type=F32) + bg_ref[...])
    ple = gate * jnp.dot(p_ref[...].astype(BF16), wp_ref[...], preferred_element_type=F32)
    z = ALPHA * x + y + ple
    mu = jnp.mean(z, axis=-1, keepdims=True)
    zc = z - mu
    var = jnp.mean(zc * zc, axis=-1, keepdims=True)
    out = zc * lax.rsqrt(var + LN_EPS) * lg_ref[...] + lb_ref[...]
    o_ref[...] = out
    ob_ref[...] = out.astype(BF16)


def _out_ln(x2d, p3, ya2d, yb2d, pj1, wo, wg, bg3, wp, lg3, lb3, layer):
    m = x2d.shape[0]
    row = lambda i: (i, 0)
    fixed = lambda i: (layer, 0, 0)
    once = pl.Buffered(1)
    return pl.pallas_call(
        _out_ln_kernel,
        grid=(m // OUT_TM,),
        in_specs=[
            pl.BlockSpec((OUT_TM, D_MODEL), row),
            pl.BlockSpec((None, OUT_TM, D_PLE), lambda i: (layer, i, 0)),
            pl.BlockSpec((OUT_TM, D_A), row),
            pl.BlockSpec((OUT_TM, D_B), row),
            pl.BlockSpec((OUT_TM, D_A), lambda i: (i, _GA_BLK)),
            pl.BlockSpec((OUT_TM, D_B), lambda i: (i, _GB_BLK)),
            pl.BlockSpec((None, D_MODEL, D_MODEL), fixed, pipeline_mode=once),
            pl.BlockSpec((None, D_MODEL, D_MODEL), fixed, pipeline_mode=once),
            pl.BlockSpec((None, 1, D_MODEL), fixed),
            pl.BlockSpec((None, D_PLE, D_MODEL), fixed, pipeline_mode=once),
            pl.BlockSpec((None, 1, D_MODEL), fixed),
            pl.BlockSpec((None, 1, D_MODEL), fixed),
        ],
        out_specs=[pl.BlockSpec((OUT_TM, D_MODEL), row), pl.BlockSpec((OUT_TM, D_MODEL), row)],
        out_shape=[jax.ShapeDtypeStruct((m, D_MODEL), F32), jax.ShapeDtypeStruct((m, D_MODEL), BF16)],
        compiler_params=pltpu.CompilerParams(
            dimension_semantics=("parallel",), vmem_limit_bytes=_VMEM_LIMIT),
        name="out_ln",
    )(x2d, p3, ya2d, yb2d, pj1, pj1, wo, wg, bg3, wp, lg3, lb3)


def _rope_col_scale():
    sc = np.ones((1, D_P2), np.float32)
    sc[:, :D_B] = Q_SCALE
    qi0 = D_B + 2 * HEAD_DIM
    sc[:, qi0:qi0 + H_IDX * D_IDX] = QI_SCALE
    wi0 = qi0 + H_IDX * D_IDX + D_IDX
    sc[:, wi0:wi0 + H_IDX] = WI_SCALE
    return jnp.asarray(sc)


def _rope_tables(positions):
    def cos_sin(dim):
        inv = ROPE_THETA ** (-jnp.arange(0, dim, 2, dtype=F32) / dim)
        ang = positions.astype(F32)[..., None] * inv
        ang = jnp.concatenate([ang, ang], -1).reshape(-1, dim)
        return jnp.cos(ang), jnp.sin(ang)

    cos_h, sin_h = cos_sin(HEAD_DIM)
    sin_h = jnp.where(jnp.arange(LANES) < HEAD_DIM // 2, -sin_h, sin_h)
    cos_i, sin_i = cos_sin(D_IDX)
    return jnp.concatenate([cos_h, sin_h, cos_i, cos_i, sin_i, sin_i], axis=-1)


def _band_bias_diagonals(rel_bias):
    lead = rel_bias.shape[:-1]
    n_far = 3 * A_TQ - 1 - REL_CLIP
    n_near = A_DIAG - n_far - (2 * REL_CLIP + 1)
    return jnp.concatenate([
        jnp.broadcast_to(rel_bias[..., -1:], lead + (n_far,)),
        rel_bias[..., ::-1],
        jnp.broadcast_to(rel_bias[..., :1], lead + (n_near,))], axis=-1) * LOG2E


def kernel(x, p, positions, w_in, b_in, rel_bias, w_out, w_ple, w_ple_gate, b_ple_gate, ln_g, ln_b):
    b, s, d = x.shape
    m = b * s
    assert d == D_MODEL and s % B_KSTEP == 0 and m % P1_TM == 0
    topk = min(TOPK_MAX, s // 4)
    tabs = _rope_tables(positions)
    col_scale = _rope_col_scale()
    diag = _band_bias_diagonals(rel_bias)
    w_in_t = jnp.swapaxes(w_in, 1, 2)
    b_in3 = b_in[:, None, :]
    wo, wg, wp = w_out.astype(BF16), w_ple_gate.astype(BF16), w_ple.astype(BF16)
    bg3, lg3, lb3 = b_ple_gate[:, None, :], ln_g[:, None, :], ln_b[:, None, :]
    p3 = p.reshape(DEPTH, m, D_PLE)
    x2d = x.reshape(m, d)
    xb = x2d.astype(BF16)
    for i in range(DEPTH):
        pj1 = _proj_plain(xb, w_in_t, b_in3, i)
        pj2 = _proj_rope(xb, w_in_t, b_in3, col_scale, tabs, i)
        ya = _attn_a(pj1.reshape(b, s, D_P1), diag, i)
        yb = _attn_b(pj2.reshape(b, s, D_P2), topk)
        x2d, xb = _out_ln(x2d, p3, ya.reshape(m, D_A), yb.reshape(m, D_B), pj1,
                          wo, wg, bg3, wp, lg3, lb3, i)
    return x2d.reshape(b, s, d)
```
